```python
import jax, jax.numpy as jnp
from jax import lax
import numpy as np

D_MODEL = 1024
BATCH = 8
SEQ = 2048
DEPTH = 1
DEC_BATCH = 128
DEC_SEQ = 1
PAST_LEN = 16384
PAGE_SIZE = 128

D_A = D_MODEL // 2
CHUNK = 128
N_HEADS_A = 4
HEAD_A = D_A // N_HEADS_A
D_B = D_MODEL - D_A
HEAD_B = 64
N_HEADS_B = D_B // HEAD_B
D_DECAY_LORA = 64
D_AAA_LORA = 64
D_GATE_LORA = 128
D_SHIFT = 3 * D_B + D_DECAY_LORA + D_AAA_LORA + D_GATE_LORA
D_IN = 2 * D_A + D_SHIFT
N_MEM = 256
N_HEADS_X = 4
HEAD_X = D_MODEL // N_HEADS_X
D_FF = 4 * D_MODEL
RMS_EPS = 1e-6
LN_EPS = 1e-5
GN_EPS = 64e-5

kernel_name = "hybrid_chunkgmlp_rwkv7_memxattn_step"

F32 = jnp.float32


def rmsnorm(x, g):
    x32 = x.astype(F32)
    y = x32 * lax.rsqrt(jnp.mean(x32 * x32, axis=-1, keepdims=True) + RMS_EPS)
    return (y * g.astype(F32)).astype(x.dtype)


def headnorm(x, g, b, eps):
    H, P = x.shape[-2:]
    x32 = x.astype(F32)
    mu = jnp.mean(x32, axis=-1, keepdims=True)
    xc = x32 - mu
    var = jnp.mean(xc * xc, axis=-1, keepdims=True)
    y = xc * lax.rsqrt(var + eps)
    return (y * g.astype(F32).reshape(H, P) + b.astype(F32).reshape(H, P)).astype(x.dtype)


def chunk_spatial_gate(u, vn, ws, bs):
    bsz, L, H, P = u.shape
    n_chunks = -(-L // CHUNK)
    pad = n_chunks * CHUNK - L
    vp = jnp.pad(vn, ((0, 0), (0, pad), (0, 0), (0, 0))).reshape(bsz, n_chunks, CHUNK, H, P)
    mask = jnp.tril(jnp.ones((CHUNK, CHUNK), dtype=bool))
    ws_c = jnp.where(mask[None], ws, 0).astype(vn.dtype)
    mixed = jnp.einsum('hts,bcshp->bcthp', ws_c, vp) + jnp.transpose(bs)[None, None, :, :, None]
    mixed = mixed.reshape(bsz, n_chunks * CHUNK, H, P)[:, :L]
    return u * mixed


def token_shift(p, prev_row, mu):
    prev = jnp.concatenate([prev_row[:, None].astype(p.dtype), p[:, :-1]], axis=1)
    return p + (prev - p) * mu, p[:, -1]


def wkv7_scan(r, decay, k, v, kk, a, S0):
    def step(S, inp):
        r_t, d_t, k_t, v_t, kk_t, a_t = inp
        sa = jnp.einsum('bhvk,bhk->bhv', S, -kk_t)
        S = S * d_t[:, :, None, :] + sa[..., None] * (kk_t * a_t)[:, :, None, :] + v_t[..., None] * k_t[:, :, None, :]
        y = jnp.einsum('bhvk,bhk->bhv', S, r_t)
        return S, y
    xs = tuple(jnp.swapaxes(t, 0, 1) for t in (r, decay, k, v, kk, a))
    S, ys = lax.scan(step, S0, xs)
    return jnp.swapaxes(ys, 0, 1), S


def parallel_mixer(h, shift0, wkv0, w_in, mu, gm_ln_g, gm_ln_b, gm_ws, gm_bs,
                   w0, w2, a0, a2, g2, k_k, k_a, r_k, ln_g, ln_b, w_out):
    bsz, L, _ = h.shape
    proj = h @ w_in
    z_a = jax.nn.gelu(proj[..., :2 * D_A], approximate=False)
    u = z_a[..., :D_A].reshape(bsz, L, N_HEADS_A, HEAD_A)
    vn = headnorm(z_a[..., D_A:].reshape(bsz, L, N_HEADS_A, HEAD_A), gm_ln_g, gm_ln_b, LN_EPS)
    out_a = chunk_spatial_gate(u, vn, gm_ws, gm_bs).reshape(bsz, L, D_A)
    chunk_start = ((L - 1) // CHUNK) * CHUNK
    chunk_v = vn[:, chunk_start:]
    pb, shift_new = token_shift(proj[..., 2 * D_A:], shift0, mu)
    pb = pb.astype(F32)
    offs = [D_B, 2 * D_B, 3 * D_B, 3 * D_B + D_DECAY_LORA, 3 * D_B + D_DECAY_LORA + D_AAA_LORA]
    r, k, v, wl, al, gl = jnp.split(pb, offs, axis=-1)
    heads = lambda t: t.reshape(bsz, L, N_HEADS_B, HEAD_B)
    w = -jax.nn.softplus(-(w0 + jnp.tanh(wl) @ w2)) - 0.5
    decay = jnp.exp(-jnp.exp(w))
    a = jax.nn.sigmoid(a0 + al @ a2)
    g = jax.nn.sigmoid(gl) @ g2
    kk = heads(k * k_k)
    kk = kk / jnp.maximum(jnp.sqrt(jnp.sum(kk * kk, axis=-1, keepdims=True)), 1e-12)
    k = k * (1.0 + (a - 1.0) * k_a)
    r_h, k_h, v_h = heads(r), heads(k), heads(v)
    yb, wkv_new = wkv7_scan(r_h, heads(decay), k_h, v_h, kk, heads(a), wkv0.astype(F32))
    yb = headnorm(yb, ln_g, ln_b, GN_EPS)
    bonus = jnp.sum(r_h * k_h * r_k.astype(F32), axis=-1, keepdims=True) * v_h
    out_b = ((yb + bonus).reshape(bsz, L, D_B) * g).astype(h.dtype)
    y = jnp.concatenate([out_a, out_b], axis=-1) @ w_out
    return y, chunk_v, shift_new, wkv_new


def memory_kv(mem, g, w_k, w_v):
    bsz = mem.shape[0]
    mn = rmsnorm(mem, g)
    mk = (mn @ w_k).reshape(bsz, N_MEM, N_HEADS_X, HEAD_X)
    mv = (mn @ w_v).reshape(bsz, N_MEM, N_HEADS_X, HEAD_X)
    return mk, mv


def cross_attn(h, mem_k, mem_v, w_q, w_o):
    bsz, L, _ = h.shape
    q = (h @ w_q).reshape(bsz, L, N_HEADS_X, HEAD_X)
    s = jnp.einsum('blhd,bmhd->bhlm', q.astype(F32), mem_k.astype(F32)) * (HEAD_X ** -0.5)
    p = jax.nn.softmax(s, axis=-1)
    o = jnp.einsum('bhlm,bmhd->blhd', p, mem_v.astype(F32)).astype(h.dtype)
    return o.reshape(bsz, L, D_MODEL) @ w_o


def sq_relu_ffn(h, w_up, w_down):
    return jnp.square(jax.nn.relu(h @ w_up)) @ w_down


def setup_inputs(seed: int = 0) -> dict:
    key = jax.random.key(seed)
    ks = iter(jax.random.split(key, 48))
    nrm = lambda shape, scale: scale * jax.random.normal(next(ks), shape, F32)
    L = DEPTH
    return {
        "x_prompt": nrm((BATCH, SEQ, D_MODEL), 1.0),
        "x_sample": nrm((DEC_BATCH, DEC_SEQ, D_MODEL), 1.0),
        "mem_prompt": nrm((BATCH, N_MEM, D_MODEL), 1.0),
        "cache_mem_k": nrm((L, DEC_BATCH, N_MEM, N_HEADS_X, HEAD_X), 1.0),
        "cache_mem_v": nrm((L, DEC_BATCH, N_MEM, N_HEADS_X, HEAD_X), 1.0),
        "state_shift": nrm((L, DEC_BATCH, D_SHIFT), 1.0),
        "state_wkv": nrm((L, DEC_BATCH, N_HEADS_B, HEAD_B, HEAD_B), 0.5),
        "norm_mix_g": 1.0 + nrm((L, D_MODEL), 0.1),
        "w_in": nrm((L, D_MODEL, D_IN), D_MODEL ** -0.5),
        "tshift_mu": jax.random.uniform(next(ks), (L, D_SHIFT), F32),
        "gm_ln_g": 1.0 + nrm((L, D_A), 0.1),
        "gm_ln_b": nrm((L, D_A), 0.01),
        "gm_ws": nrm((L, N_HEADS_A, CHUNK, CHUNK), CHUNK ** -0.5),
        "gm_bs": 1.0 + nrm((L, N_HEADS_A, CHUNK), 0.1),
        "rw_w0": -1.0 + nrm((L, D_B), 0.5),
        "rw_w2": nrm((L, D_DECAY_LORA, D_B), 0.1 * D_DECAY_LORA ** -0.5),
        "rw_a0": nrm((L, D_B), 0.1),
        "rw_a2": nrm((L, D_AAA_LORA, D_B), D_AAA_LORA ** -0.5),
        "rw_g2": nrm((L, D_GATE_LORA, D_B), D_GATE_LORA ** -0.5),
        "rw_kk": 0.85 + nrm((L, D_B), 0.05),
        "rw_ka": 1.0 + nrm((L, D_B), 0.05),
        "rw_rk": nrm((L, N_HEADS_B, HEAD_B), 0.1),
        "rw_ln_g": 1.0 + nrm((L, D_B), 0.1),
        "rw_ln_b": nrm((L, D_B), 0.01),
        "w_out": nrm((L, D_MODEL, D_MODEL), D_MODEL ** -0.5),
        "norm_x_g": 1.0 + nrm((L, D_MODEL), 0.1),
        "norm_mem_g": 1.0 + nrm((L, D_MODEL), 0.1),
        "w_xq": nrm((L, D_MODEL, D_MODEL), D_MODEL ** -0.5),
        "w_xk": nrm((L, D_MODEL, D_MODEL), D_MODEL ** -0.5),
        "w_xv": nrm((L, D_MODEL, D_MODEL), D_MODEL ** -0.5),
        "w_xo": nrm((L, D_MODEL, D_MODEL), D_MODEL ** -0.5),
        "norm_ffn_g": 1.0 + nrm((L, D_MODEL), 0.1),
        "w_up": nrm((L, D_MODEL, D_FF), D_MODEL ** -0.5),
        "w_down": nrm((L, D_FF, D_MODEL), D_FF ** -0.5),
        "final_g": 1.0 + nrm((D_MODEL,), 0.1),
    }


def reference(x_prompt, x_sample, mem_prompt, cache_mem_k, cache_mem_v, state_shift, state_wkv,
              norm_mix_g, w_in, tshift_mu, gm_ln_g, gm_ln_b, gm_ws, gm_bs,
              rw_w0, rw_w2, rw_a0, rw_a2, rw_g2, rw_kk, rw_ka, rw_rk, rw_ln_g, rw_ln_b, w_out,
              norm_x_g, norm_mem_g, w_xq, w_xk, w_xv, w_xo, norm_ffn_g, w_up, w_down, final_g):

    def layer(l, x, mem_k, mem_v, shift0, wkv0):
        y_mix, chunk_v, shift_new, wkv_new = parallel_mixer(
            rmsnorm(x, norm_mix_g[l]), shift0, wkv0, w_in[l], tshift_mu[l],
            gm_ln_g[l], gm_ln_b[l], gm_ws[l], gm_bs[l],
            rw_w0[l], rw_w2[l], rw_a0[l], rw_a2[l], rw_g2[l], rw_kk[l], rw_ka[l], rw_rk[l],
            rw_ln_g[l], rw_ln_b[l], w_out[l])
        x = x + y_mix
        x = x + cross_attn(rmsnorm(x, norm_x_g[l]), mem_k, mem_v, w_xq[l], w_xo[l])
        x = x + sq_relu_ffn(rmsnorm(x, norm_ffn_g[l]), w_up[l], w_down[l])
        return x, chunk_v, shift_new, wkv_new

    bp = x_prompt.shape[0]
    xp = x_prompt
    mk_p, mv_p, sh_p, wk_p, cv_p = [], [], [], [], []
    for l in range(DEPTH):
        mk, mv = memory_kv(mem_prompt, norm_mem_g[l], w_xk[l], w_xv[l])
        shift0 = jnp.zeros((bp, D_SHIFT), x_prompt.dtype)
        wkv0 = jnp.zeros((bp, N_HEADS_B, HEAD_B, HEAD_B), F32)
        xp, cv, sh, wk = layer(l, xp, mk, mv, shift0, wkv0)
        mk_p.append(mk); mv_p.append(mv); sh_p.append(sh); wk_p.append(wk); cv_p.append(cv)

    xs = x_sample
    sh_s, wk_s, cv_s = [], [], []
    for l in range(DEPTH):
        xs, cv, sh, wk = layer(l, xs, cache_mem_k[l], cache_mem_v[l], state_shift[l], state_wkv[l])
        sh_s.append(sh); wk_s.append(wk); cv_s.append(cv)

    y_prompt = rmsnorm(xp, final_g)
    y_sample = rmsnorm(xs, final_g)
    return (y_prompt, y_sample,
            jnp.stack(mk_p), jnp.stack(mv_p), jnp.stack(sh_p), jnp.stack(wk_p), jnp.stack(cv_p),
            jnp.stack(sh_s), jnp.stack(wk_s), jnp.stack(cv_s))
```

```python
import functools

import numpy as np
import jax
import jax.numpy as jnp
from jax import lax
from jax.experimental import pallas as pl
from jax.experimental.pallas import tpu as pltpu

F32 = jnp.float32
BF16 = jnp.bfloat16

CHUNK = 128
N_HEADS_A = 4
HEAD_A = 128
HEAD_B = 64
N_HEADS_B = 8
D_DECAY_LORA = 64
D_AAA_LORA = 64
D_GATE_LORA = 128
N_HEADS_X = 4
RMS_EPS = 1e-6
LN_EPS = 1e-5
GN_EPS = 64e-5

VMEM_LIMIT_BYTES = 56 * 1024 * 1024
SEG_LANES = 256
SUBLANES = 8


def _cparams(sem):
    return pltpu.CompilerParams(dimension_semantics=sem, vmem_limit_bytes=VMEM_LIMIT_BYTES)


def _full(shape):
    nd = len(shape)
    return pl.BlockSpec(shape, lambda *_: (0,) * nd)


def _rms(x, g):
    return x * lax.rsqrt(jnp.mean(x * x, axis=-1, keepdims=True) + RMS_EPS) * g


def _bdot(a, w):
    return jnp.dot(a.astype(BF16), w, preferred_element_type=F32)


def _segsum(x, ones_blk, passes):
    outs = []
    for j in range(x.shape[1] // SEG_LANES):
        rem = x[:, j * SEG_LANES:(j + 1) * SEG_LANES]
        acc = None
        for p in range(passes):
            hi = rem.astype(BF16)
            t = jnp.dot(hi, ones_blk, preferred_element_type=F32)
            acc = t if acc is None else acc + t
            if p + 1 < passes:
                rem = rem - hi.astype(F32)
        outs.append(acc)
    return jnp.concatenate(outs, axis=1)


def _gelu(x):
    return 0.5 * x * (1.0 + lax.erf(x * np.float32(np.sqrt(0.5))))


def _softplus(x):
    return jnp.maximum(x, 0.0) + jnp.log1p(jnp.exp(-jnp.abs(x)))


def _headnorm_a(z, g, b):
    outs = []
    for h in range(N_HEADS_A):
        zh = z[:, h * HEAD_A:(h + 1) * HEAD_A]
        mu = jnp.mean(zh, axis=-1, keepdims=True)
        zc = zh - mu
        var = jnp.mean(zc * zc, axis=-1, keepdims=True)
        outs.append(zc * lax.rsqrt(var + LN_EPS))
    return jnp.concatenate(outs, axis=1) * g + b


def _rwkv_prep(pb, ones_blk, w0, w2p, a0, a2p, g2, kkw, kaw, rkw):
    d_b = N_HEADS_B * HEAD_B
    r = pb[:, 0:d_b]
    k = pb[:, d_b:2 * d_b]
    v = pb[:, 2 * d_b:3 * d_b]
    lora_in = pb[:, 3 * d_b:3 * d_b + D_DECAY_LORA + D_AAA_LORA]
    gate_in = pb[:, 3 * d_b + D_DECAY_LORA + D_AAA_LORA:]
    w = -_softplus(-(w0 + _bdot(jnp.tanh(lora_in), w2p))) - 0.5
    decay = jnp.exp(-jnp.exp(w))
    a = jax.nn.sigmoid(a0 + _bdot(lora_in, a2p))
    g = _bdot(jax.nn.sigmoid(gate_in), g2)
    kk = k * kkw
    kk = kk / jnp.maximum(jnp.sqrt(_segsum(kk * kk, ones_blk, 3)), 1e-12)
    k2 = k * (1.0 + (a - 1.0) * kaw)
    bonus = _segsum(r * k2 * rkw, ones_blk, 3) * v
    return r, k2, v, decay, kk, kk * a, g, bonus


def _memkv_kernel(mem_ref, g_ref, wk_ref, wv_ref, mk_ref, mv_ref):
    mn = _rms(mem_ref[...], g_ref[...]).astype(BF16)
    mk_ref[...] = jnp.dot(mn, wk_ref[...], preferred_element_type=F32)
    mv_ref[...] = jnp.dot(mn, wv_ref[...], preferred_element_type=F32)


def _memory_kv(mem2d, g, wk, wv):
    m, d = mem2d.shape
    tm = min(512, m)
    return pl.pallas_call(
        _memkv_kernel,
        grid=(m // tm,),
        in_specs=[pl.BlockSpec((tm, d), lambda i: (i, 0)), _full((1, d)), _full((d, d)), _full((d, d))],
        out_specs=[pl.BlockSpec((tm, d), lambda i: (i, 0))] * 2,
        out_shape=[jax.ShapeDtypeStruct((m, d), F32)] * 2,
        compiler_params=_cparams(("arbitrary",)),
        name="memory_kv",
    )(mem2d, g, wk, wv)


def _inproj_prompt_kernel(x_ref, shift0_ref, g_ref, win_ref, mu_ref, lng_ref, lnb_ref, ws_ref, bsb_ref,
                          w0_ref, w2p_ref, a0_ref, a2p_ref, g2_ref, kkw_ref, kaw_ref, rkw_ref, ones_ref,
                          outa_ref, cv_ref, shift_ref,
                          r_ref, k_ref, v_ref, d_ref, kk_ref, b_ref, gate_ref, bonus_ref,
                          carry_ref, *, tl):
    l = pl.program_id(1)
    last = pl.num_programs(1) - 1
    d_a = N_HEADS_A * HEAD_A

    @pl.when(l == 0)
    def _():
        carry_ref[...] = shift0_ref[0]

    h = _rms(x_ref[0], g_ref[...])
    proj = _bdot(h, win_ref[...])

    z = _gelu(proj[:, :2 * d_a])
    u = z[:, :d_a]
    vn = _headnorm_a(z[:, d_a:], lng_ref[...], lnb_ref[...])
    row = lax.broadcasted_iota(jnp.int32, (CHUNK, CHUNK), 0)
    col = lax.broadcasted_iota(jnp.int32, (CHUNK, CHUNK), 1)
    tril = row >= col
    for h_i in range(N_HEADS_A):
        w_h = jnp.where(tril, ws_ref[h_i], 0.0).astype(BF16)
        lanes = slice(h_i * HEAD_A, (h_i + 1) * HEAD_A)
        for c in range(tl // CHUNK):
            rows = slice(c * CHUNK, (c + 1) * CHUNK)
            mixed = jnp.dot(w_h, vn[rows, lanes].astype(BF16), preferred_element_type=F32) + bsb_ref[h_i]
            outa_ref[0, rows, lanes] = u[rows, lanes] * mixed

    @pl.when(l == last)
    def _():
        cv_ref[0] = vn[tl - CHUNK:, :]

    p = proj[:, 2 * d_a:]
    rolled = pltpu.roll(p, 1, axis=0)
    first = lax.broadcasted_iota(jnp.int32, p.shape, 0) == 0
    prev = jnp.where(first, jnp.broadcast_to(carry_ref[...], p.shape), rolled)
    carry_ref[...] = p[tl - 1:tl, :]

    @pl.when(l == last)
    def _():
        shift_ref[0] = p[tl - 1:tl, :]

    pb = p + (prev - p) * mu_ref[...]
    outs = _rwkv_prep(pb, ones_ref[...], w0_ref[...], w2p_ref[...], a0_ref[...], a2p_ref[...], g2_ref[...],
                      kkw_ref[...], kaw_ref[...], rkw_ref[...])
    for ref, val in zip((r_ref, k_ref, v_ref, d_ref, kk_ref, b_ref, gate_ref, bonus_ref), outs):
        ref[0] = val


def _inproj_prompt(x, shift0, wts, tl):
    bsz, seq, d = x.shape
    d_a = N_HEADS_A * HEAD_A
    d_b = N_HEADS_B * HEAD_B
    d_in = wts["w_in"].shape[1]
    d_shift = d_in - 2 * d_a
    tok = lambda w: pl.BlockSpec((1, tl, w), lambda b, l: (b, l, 0))
    per_b = lambda r, w: pl.BlockSpec((1, r, w), lambda b, l: (b, 0, 0))
    in_specs = [
        tok(d), per_b(1, d_shift), _full((1, d)), _full((d, d_in)), _full((1, d_shift)),
        _full((1, d_a)), _full((1, d_a)), _full((N_HEADS_A, CHUNK, CHUNK)), _full((N_HEADS_A, CHUNK, HEAD_A)),
        _full((1, d_b)), _full((D_DECAY_LORA + D_AAA_LORA, d_b)), _full((1, d_b)),
        _full((D_DECAY_LORA + D_AAA_LORA, d_b)), _full((D_GATE_LORA, d_b)),
        _full((1, d_b)), _full((1, d_b)), _full((1, d_b)), _full((SEG_LANES, SEG_LANES)),
    ]
    out_specs = [tok(d_a), per_b(CHUNK, d_a), per_b(1, d_shift)] + [tok(d_b)] * 8
    out_shape = ([jax.ShapeDtypeStruct((bsz, seq, d_a), F32),
                  jax.ShapeDtypeStruct((bsz, CHUNK, d_a), F32),
                  jax.ShapeDtypeStruct((bsz, 1, d_shift), F32)]
                 + [jax.ShapeDtypeStruct((bsz, seq, d_b), F32)] * 8)
    return pl.pallas_call(
        functools.partial(_inproj_prompt_kernel, tl=tl),
        grid=(bsz, seq // tl),
        in_specs=in_specs, out_specs=out_specs, out_shape=out_shape,
        scratch_shapes=[pltpu.VMEM((1, d_shift), F32)],
        compiler_params=_cparams(("arbitrary", "arbitrary")),
        name="inproj_prompt",
    )(x, shift0, wts["norm_mix_g"], wts["w_in"], wts["mu"], wts["gm_ln_g"], wts["gm_ln_b"], wts["gm_ws"],
      wts["bs_b"], wts["w0"], wts["w2p"], wts["a0"], wts["a2p"], wts["g2"], wts["kkw"], wts["kaw"], wts["rkw"],
      wts["ones"])


def _inproj_sample_kernel(x_ref, shift0_ref, g_ref, win_ref, mu_ref, lng_ref, lnb_ref, ws0_ref, bs0_ref,
                          w0_ref, w2p_ref, a0_ref, a2p_ref, g2_ref, kkw_ref, kaw_ref, rkw_ref, ones_ref,
                          outa_ref, cv_ref, shift_ref,
                          r_ref, k_ref, v_ref, d_ref, kk_ref, b_ref, gate_ref, bonus_ref):
    d_a = N_HEADS_A * HEAD_A
    h = _rms(x_ref[...], g_ref[...])
    proj = _bdot(h, win_ref[...])
    z = _gelu(proj[:, :2 * d_a])
    vn = _headnorm_a(z[:, d_a:], lng_ref[...], lnb_ref[...])
    outa_ref[...] = z[:, :d_a] * (ws0_ref[...] * vn + bs0_ref[...])
    cv_ref[...] = vn
    p = proj[:, 2 * d_a:]
    shift_ref[...] = p
    pb = p + (shift0_ref[...] - p) * mu_ref[...]
    outs = _rwkv_prep(pb, ones_ref[...], w0_ref[...], w2p_ref[...], a0_ref[...], a2p_ref[...], g2_ref[...],
                      kkw_ref[...], kaw_ref[...], rkw_ref[...])
    for ref, val in zip((r_ref, k_ref, v_ref, d_ref, kk_ref, b_ref, gate_ref, bonus_ref), outs):
        ref[...] = val


def _inproj_sample(x, shift0, wts):
    n, d = x.shape
    d_a = N_HEADS_A * HEAD_A
    d_b = N_HEADS_B * HEAD_B
    d_in = wts["w_in"].shape[1]
    d_shift = d_in - 2 * d_a
    args = (x, shift0, wts["norm_mix_g"], wts["w_in"], wts["mu"], wts["gm_ln_g"], wts["gm_ln_b"], wts["ws0"],
            wts["bs0"], wts["w0"], wts["w2p"], wts["a0"], wts["a2p"], wts["g2"], wts["kkw"], wts["kaw"],
            wts["rkw"], wts["ones"])
    out_shape = ([jax.ShapeDtypeStruct((n, d_a), F32), jax.ShapeDtypeStruct((n, d_a), F32),
                  jax.ShapeDtypeStruct((n, d_shift), F32)] + [jax.ShapeDtypeStruct((n, d_b), F32)] * 8)
    return pl.pallas_call(
        _inproj_sample_kernel,
        grid=(1,),
        in_specs=[_full(a.shape) for a in args],
        out_specs=[_full(s.shape) for s in out_shape],
        out_shape=out_shape,
        compiler_params=_cparams(("arbitrary",)),
        name="inproj_sample",
    )(*args)


def _wkv_kernel(r_ref, k_ref, v_ref, d_ref, kk_ref, b_ref, s0_ref, ones_ref, y_ref, sout_ref, s_scr, *, bb, tb):
    l = pl.program_id(1)
    d_b = N_HEADS_B * HEAD_B

    @pl.when(l == 0)
    def _():
        s_scr[...] = s0_ref[...]

    ones_blk = ones_ref[...]
    sub = lax.broadcasted_iota(jnp.int32, (HEAD_B, d_b), 0)
    lane = lax.broadcasted_iota(jnp.int32, (HEAD_B, d_b), 1)
    diag = jnp.where(lane % HEAD_B == sub, 1.0, 0.0).astype(F32)

    def step(t, carry):
        row = lambda ref, bi: jnp.broadcast_to(ref[bi, pl.ds(t, 1), :], (HEAD_B, d_b))
        s_old = [s_scr[bi] for bi in range(bb)]
        p = jnp.concatenate([s_old[bi] * row(kk_ref, bi) for bi in range(bb)], axis=0)
        e = jnp.concatenate([diag * row(v_ref, bi) for bi in range(bb)], axis=0)
        sa = _segsum(p, ones_blk, 2)
        vcol = _segsum(e, ones_blk, 2)
        q = []
        for bi in range(bb):
            rows = slice(bi * HEAD_B, (bi + 1) * HEAD_B)
            s_new = s_old[bi] * row(d_ref, bi) - sa[rows] * row(b_ref, bi) + vcol[rows] * row(k_ref, bi)
            s_scr[bi] = s_new
            q.append(s_new * row(r_ref, bi))
        yb = _segsum(jnp.concatenate(q, axis=0), ones_blk, 1)
        for bi in range(bb):
            rows = slice(bi * HEAD_B, (bi + 1) * HEAD_B)
            y_ref[bi, pl.ds(t, 1), :] = jnp.sum(diag * yb[rows], axis=0, keepdims=True)
        return carry

    lax.fori_loop(0, tb, step, 0)

    @pl.when(l == pl.num_programs(1) - 1)
    def _():
        sout_ref[...] = s_scr[...]


def _wkv(r, k, v, d, kk, b, s0, ones_blk, bb, tb):
    bsz, seq, d_b = r.shape
    tok = pl.BlockSpec((bb, tb, d_b), lambda i, l: (i, l, 0))
    st = pl.BlockSpec((bb, HEAD_B, d_b), lambda i, l: (i, 0, 0))
    return pl.pallas_call(
        functools.partial(_wkv_kernel, bb=bb, tb=tb),
        grid=(bsz // bb, seq // tb),
        in_specs=[tok] * 6 + [st, _full((SEG_LANES, SEG_LANES))],
        out_specs=[tok, st],
        out_shape=[jax.ShapeDtypeStruct((bsz, seq, d_b), F32), jax.ShapeDtypeStruct((bsz, HEAD_B, d_b), F32)],
        scratch_shapes=[pltpu.VMEM((bb, HEAD_B, d_b), F32)],
        compiler_params=_cparams(("arbitrary", "arbitrary")),
        name="wkv",
    )(r, k, v, d, kk, b, s0, ones_blk)


def _mix_out(x, outa, y, gate, bonus, lng, lnb, wout, ones_blk):
    d_a = outa.shape[-1]
    mu = _segsum(y, ones_blk, 3) * (1.0 / HEAD_B)
    yc = y - mu
    var = _segsum(yc * yc, ones_blk, 3) * (1.0 / HEAD_B)
    yn = yc * lax.rsqrt(var + GN_EPS) * lng + lnb
    outb = (yn + bonus) * gate
    return x + _bdot(outa, wout[:d_a]) + _bdot(outb, wout[d_a:])


def _softmax(s):
    m = jnp.max(s, axis=-1, keepdims=True)
    e = jnp.exp(s - m)
    return e / jnp.sum(e, axis=-1, keepdims=True)


def _mixattn_prompt_kernel(x_ref, outa_ref, y_ref, gate_ref, bonus_ref, mk_ref, mv_ref,
                           lng_ref, lnb_ref, wout_ref, gx_ref, wq_ref, ones_ref, x1_ref, o_ref):
    x1 = _mix_out(x_ref[0], outa_ref[0], y_ref[0], gate_ref[0], bonus_ref[0], lng_ref[...], lnb_ref[...],
                  wout_ref[...], ones_ref[...])
    x1_ref[0] = x1
    q = _bdot(_rms(x1, gx_ref[...]), wq_ref[...])
    hx = q.shape[1] // N_HEADS_X
    for h_i in range(N_HEADS_X):
        lanes = slice(h_i * hx, (h_i + 1) * hx)
        s = lax.dot_general(q[:, lanes].astype(BF16), mk_ref[0, :, lanes], (((1,), (1,)), ((), ())),
                            preferred_element_type=F32) * (hx ** -0.5)
        o_ref[0, :, lanes] = _bdot(_softmax(s), mv_ref[0, :, lanes]).astype(BF16)


def _mixattn_prompt(x, outa, y, gate, bonus, mk, mv, wts, tl):
    bsz, seq, d = x.shape
    d_a = outa.shape[-1]
    d_b = y.shape[-1]
    n_mem = mk.shape[1]
    tok = lambda w: pl.BlockSpec((1, tl, w), lambda b, l: (b, l, 0))
    mem = pl.BlockSpec((1, n_mem, d), lambda b, l: (b, 0, 0))
    return pl.pallas_call(
        _mixattn_prompt_kernel,
        grid=(bsz, seq // tl),
        in_specs=[tok(d), tok(d_a), tok(d_b), tok(d_b), tok(d_b), mem, mem,
                  _full((1, d_b)), _full((1, d_b)), _full((d, d)), _full((1, d)), _full((d, d)),
                  _full((SEG_LANES, SEG_LANES))],
        out_specs=[tok(d), tok(d)],
        out_shape=[jax.ShapeDtypeStruct((bsz, seq, d), F32), jax.ShapeDtypeStruct((bsz, seq, d), BF16)],
        compiler_params=_cparams(("arbitrary", "arbitrary")),
        name="mixattn_prompt",
    )(x, outa, y, gate, bonus, mk, mv, wts["rw_ln_g"], wts["rw_ln_b"], wts["w_out"], wts["norm_x_g"],
      wts["w_xq"], wts["ones"])


def _mixq_sample_kernel(x_ref, outa_ref, y_ref, gate_ref, bonus_ref, lng_ref, lnb_ref, wout_ref, gx_ref, wq_ref,
                        ones_ref, x1_ref, q_ref):
    x1 = _mix_out(x_ref[...], outa_ref[...], y_ref[...], gate_ref[...], bonus_ref[...], lng_ref[...],
                  lnb_ref[...], wout_ref[...], ones_ref[...])
    x1_ref[...] = x1
    q_ref[...] = _bdot(_rms(x1, gx_ref[...]), wq_ref[...])


def _mixq_sample(x, outa, y, gate, bonus, wts):
    n, d = x.shape
    args = (x, outa, y, gate, bonus, wts["rw_ln_g"], wts["rw_ln_b"], wts["w_out"], wts["norm_x_g"], wts["w_xq"],
            wts["ones"])
    return pl.pallas_call(
        _mixq_sample_kernel,
        grid=(1,),
        in_specs=[_full(a.shape) for a in args],
        out_specs=[_full((n, d)), _full((n, d))],
        out_shape=[jax.ShapeDtypeStruct((n, d), F32)] * 2,
        compiler_params=_cparams(("arbitrary",)),
        name="mixq_sample",
    )(*args)


def _decode_attn_kernel(q_ref, k_ref, v_ref, o_ref, *, bb):
    d = q_ref.shape[1]
    hx = d // N_HEADS_X
    head_of_row = lax.broadcasted_iota(jnp.int32, (SUBLANES, d), 0)
    head_of_lane = lax.broadcasted_iota(jnp.int32, (SUBLANES, d), 1) // hx
    head_mask = jnp.where(head_of_row == head_of_lane, 1.0, 0.0).astype(F32)
    for bi in range(bb):
        q_rows = jnp.broadcast_to(q_ref[bi:bi + 1, :], (SUBLANES, d)) * head_mask
        s = lax.dot_general(q_rows.astype(BF16), k_ref[bi].astype(BF16), (((1,), (1,)), ((), ())),
                            preferred_element_type=F32) * (hx ** -0.5)
        o_all = _bdot(_softmax(s), v_ref[bi].astype(BF16))
        o_ref[bi:bi + 1, :] = jnp.sum(o_all * head_mask, axis=0, keepdims=True).astype(BF16)


def _decode_attn(q, mem_k, mem_v, bb):
    n, d = q.shape
    n_mem = mem_k.shape[1]
    kv = pl.BlockSpec((bb, n_mem, d), lambda i: (i, 0, 0))
    return pl.pallas_call(
        functools.partial(_decode_attn_kernel, bb=bb),
        grid=(n // bb,),
        in_specs=[pl.BlockSpec((bb, d), lambda i: (i, 0)), kv, kv],
        out_specs=pl.BlockSpec((bb, d), lambda i: (i, 0)),
        out_shape=jax.ShapeDtypeStruct((n, d), BF16),
        compiler_params=_cparams(("arbitrary",)),
        name="decode_attn",
    )(q, mem_k, mem_v)


def _ffn_kernel(x1_ref, o_ref, wo_ref, gf_ref, wup_ref, wdn_ref, gfin_ref, y_ref, *, ff_chunk):
    x2 = x1_ref[...] + jnp.dot(o_ref[...], wo_ref[...], preferred_element_type=F32)
    hn = _rms(x2, gf_ref[...]).astype(BF16)
    acc = x2
    for c in range(wup_ref.shape[1] // ff_chunk):
        cols = slice(c * ff_chunk, (c + 1) * ff_chunk)
        up = jnp.maximum(jnp.dot(hn, wup_ref[:, cols], preferred_element_type=F32), 0.0)
        acc = acc + _bdot(up * up, wdn_ref[cols, :])
    y_ref[...] = _rms(acc, gfin_ref[...])


def _ffn(x1, o, wts, tm):
    n, d = x1.shape
    d_ff = wts["w_up"].shape[1]
    row = pl.BlockSpec((tm, d), lambda i: (i, 0))
    return pl.pallas_call(
        functools.partial(_ffn_kernel, ff_chunk=min(1024, d_ff)),
        grid=(n // tm,),
        in_specs=[row, row, _full((d, d)), _full((1, d)), _full((d, d_ff)), _full((d_ff, d)), _full((1, d))],
        out_specs=row,
        out_shape=jax.ShapeDtypeStruct((n, d), F32),
        compiler_params=_cparams(("arbitrary",)),
        name="ffn",
    )(x1, o, wts["w_xo"], wts["norm_ffn_g"], wts["w_up"], wts["w_down"], wts["final_g"])


def _state_to_rows(s):
    bsz, nh, nv, nk = s.shape
    return jnp.transpose(s, (0, 2, 1, 3)).reshape(bsz, nv, nh * nk)


def _rows_to_state(s):
    bsz, nv, hk = s.shape
    return jnp.transpose(s.reshape(bsz, nv, N_HEADS_B, hk // N_HEADS_B), (0, 2, 1, 3))


def _layer_weights(l, norm_mix_g, w_in, tshift_mu, gm_ln_g, gm_ln_b, gm_ws, gm_bs, rw_w0, rw_w2, rw_a0, rw_a2,
                   rw_g2, rw_kk, rw_ka, rw_rk, rw_ln_g, rw_ln_b, w_out, norm_x_g, norm_mem_g, w_xq, w_xk, w_xv,
                   w_xo, norm_ffn_g, w_up, w_down, final_g):
    row = lambda a: a.reshape(1, -1).astype(F32)
    d_b = N_HEADS_B * HEAD_B
    zeros_w = jnp.zeros((D_AAA_LORA, d_b), F32)
    zeros_a = jnp.zeros((D_DECAY_LORA, d_b), F32)
    seg = np.arange(SEG_LANES) // HEAD_B
    return {
        "norm_mix_g": row(norm_mix_g[l]), "w_in": w_in[l].astype(BF16), "mu": row(tshift_mu[l]),
        "gm_ln_g": row(gm_ln_g[l]), "gm_ln_b": row(gm_ln_b[l]), "gm_ws": gm_ws[l],
        "bs_b": jnp.broadcast_to(gm_bs[l][:, :, None], (N_HEADS_A, CHUNK, HEAD_A)),
        "ws0": jnp.repeat(gm_ws[l][:, 0, 0], HEAD_A).reshape(1, -1),
        "bs0": jnp.repeat(gm_bs[l][:, 0], HEAD_A).reshape(1, -1),
        "w0": row(rw_w0[l]),
        "w2p": jnp.concatenate([rw_w2[l], zeros_w], axis=0).astype(BF16),
        "a0": row(rw_a0[l]),
        "a2p": jnp.concatenate([zeros_a, rw_a2[l]], axis=0).astype(BF16),
        "g2": rw_g2[l].astype(BF16), "kkw": row(rw_kk[l]), "kaw": row(rw_ka[l]), "rkw": row(rw_rk[l]),
        "rw_ln_g": row(rw_ln_g[l]), "rw_ln_b": row(rw_ln_b[l]), "w_out": w_out[l].astype(BF16),
        "norm_x_g": row(norm_x_g[l]), "norm_mem_g": row(norm_mem_g[l]), "w_xq": w_xq[l].astype(BF16),
        "w_xk": w_xk[l].astype(BF16), "w_xv": w_xv[l].astype(BF16), "w_xo": w_xo[l].astype(BF16),
        "norm_ffn_g": row(norm_ffn_g[l]), "w_up": w_up[l].astype(BF16), "w_down": w_down[l].astype(BF16),
        "final_g": row(final_g),
        "ones": jnp.asarray(seg[:, None] == seg[None, :], BF16),
    }


def kernel(x_prompt, x_sample, mem_prompt, cache_mem_k, cache_mem_v, state_shift, state_wkv, norm_mix_g, w_in, tshift_mu, gm_ln_g, gm_ln_b, gm_ws, gm_bs, rw_w0, rw_w2, rw_a0, rw_a2, rw_g2, rw_kk, rw_ka, rw_rk, rw_ln_g, rw_ln_b, w_out, norm_x_g, norm_mem_g, w_xq, w_xk, w_xv, w_xo, norm_ffn_g, w_up, w_down, final_g):
    depth = w_in.shape[0]
    assert depth == 1, "single-layer step"
    bsz, seq, d = x_prompt.shape
    nsmp, dec_seq, _ = x_sample.shape
    assert dec_seq == 1 and seq % CHUNK == 0
    n_mem = mem_prompt.shape[1]
    d_b = N_HEADS_B * HEAD_B
    d_shift = state_shift.shape[-1]
    wts = _layer_weights(0, norm_mix_g, w_in, tshift_mu, gm_ln_g, gm_ln_b, gm_ws, gm_bs, rw_w0, rw_w2, rw_a0,
                         rw_a2, rw_g2, rw_kk, rw_ka, rw_rk, rw_ln_g, rw_ln_b, w_out, norm_x_g, norm_mem_g,
                         w_xq, w_xk, w_xv, w_xo, norm_ffn_g, w_up, w_down, final_g)

    tl = min(256, seq)
    mk, mv = _memory_kv(mem_prompt.reshape(bsz * n_mem, d), wts["norm_mem_g"], wts["w_xk"], wts["w_xv"])
    (outa, cv_p, sh_p, r, k, v, dcy, kk, b, gate, bonus) = _inproj_prompt(
        x_prompt, jnp.zeros((bsz, 1, d_shift), F32), wts, tl)
    wkv_bb = min(8, bsz)
    y, s_p = _wkv(r, k, v, dcy, kk, b, jnp.zeros((bsz, HEAD_B, d_b), F32), wts["ones"], wkv_bb, min(64, seq))
    x1, o = _mixattn_prompt(x_prompt, outa, y, gate, bonus, mk.reshape(bsz, n_mem, d).astype(BF16),
                            mv.reshape(bsz, n_mem, d).astype(BF16), wts, tl)
    y_prompt = _ffn(x1.reshape(bsz * seq, d), o.reshape(bsz * seq, d), wts, min(512, bsz * seq))

    (outa_s, cv_s, sh_s, r, k, v, dcy, kk, b, gate_s, bonus_s) = _inproj_sample(
        x_sample.reshape(nsmp, d), state_shift[0], wts)
    tok = lambda a: a.reshape(nsmp, 1, d_b)
    y_s, s_s = _wkv(tok(r), tok(k), tok(v), tok(dcy), tok(kk), tok(b), _state_to_rows(state_wkv[0]),
                    wts["ones"], min(8, nsmp), 1)
    x1_s, q_s = _mixq_sample(x_sample.reshape(nsmp, d), outa_s, y_s.reshape(nsmp, d_b), gate_s, bonus_s, wts)
    o_s = _decode_attn(q_s, cache_mem_k[0].reshape(nsmp, n_mem, d), cache_mem_v[0].reshape(nsmp, n_mem, d),
                       min(8, nsmp))
    y_sample = _ffn(x1_s, o_s, wts, min(512, nsmp))

    mem_shape = (1, bsz, n_mem, N_HEADS_X, d // N_HEADS_X)
    return (y_prompt.reshape(bsz, seq, d), y_sample.reshape(nsmp, 1, d),
            mk.reshape(mem_shape), mv.reshape(mem_shape),
            sh_p.reshape(1, bsz, d_shift), _rows_to_state(s_p)[None],
            cv_p.reshape(1, bsz, CHUNK, N_HEADS_A, HEAD_A),
            sh_s.reshape(1, nsmp, d_shift), _rows_to_state(s_s)[None],
            cv_s.reshape(1, nsmp, 1, N_HEADS_A, HEAD_A))
```

```python
import functools

import numpy as np
import jax
import jax.numpy as jnp
from jax import lax
from jax.experimental import pallas as pl
from jax.experimental.pallas import tpu as pltpu

F32 = jnp.float32
BF16 = jnp.bfloat16

CHUNK = 128
N_HEADS_A = 4
HEAD_A = 128
HEAD_B = 64
N_HEADS_B = 8
D_DECAY_LORA = 64
D_AAA_LORA = 64
D_GATE_LORA = 128
N_HEADS_X = 4
RMS_EPS = 1e-6
LN_EPS = 1e-5
GN_EPS = 64e-5

VMEM_LIMIT_BYTES = 56 * 1024 * 1024
SEG_LANES = 256
SUBLANES = 8


def _cparams(sem):
    return pltpu.CompilerParams(dimension_semantics=sem, vmem_limit_bytes=VMEM_LIMIT_BYTES)


def _full(shape):
    nd = len(shape)
    return pl.BlockSpec(shape, lambda *_: (0,) * nd)


def _rms(x, g):
    return x * lax.rsqrt(jnp.mean(x * x, axis=-1, keepdims=True) + RMS_EPS) * g


def _bdot(a, w):
    return jnp.dot(a.astype(BF16), w, preferred_element_type=F32)


def _segsum(x, ones_blk, passes):
    outs = []
    for j in range(x.shape[1] // SEG_LANES):
        rem = x[:, j * SEG_LANES:(j + 1) * SEG_LANES]
        acc = None
        for p in range(passes):
            piece = rem if p + 1 == passes else rem.astype(BF16).astype(F32)
            t = jnp.dot(piece, ones_blk, preferred_element_type=F32)
            acc = t if acc is None else acc + t
            if p + 1 < passes:
                rem = rem - piece
        outs.append(acc)
    return jnp.concatenate(outs, axis=1)


def _gelu(x):
    return 0.5 * x * (1.0 + lax.erf(x * np.float32(np.sqrt(0.5))))


def _softplus(x):
    return jnp.maximum(x, 0.0) + jnp.log1p(jnp.exp(-jnp.abs(x)))


def _headnorm_a(z, g, b):
    outs = []
    for h in range(N_HEADS_A):
        zh = z[:, h * HEAD_A:(h + 1) * HEAD_A]
        mu = jnp.mean(zh, axis=-1, keepdims=True)
        zc = zh - mu
        var = jnp.mean(zc * zc, axis=-1, keepdims=True)
        outs.append(zc * lax.rsqrt(var + LN_EPS))
    return jnp.concatenate(outs, axis=1) * g + b


def _rwkv_prep(pb, ones_blk, w0, w2p, a0, a2p, g2, kkw, kaw, rkw):
    d_b = N_HEADS_B * HEAD_B
    r = pb[:, 0:d_b]
    k = pb[:, d_b:2 * d_b]
    v = pb[:, 2 * d_b:3 * d_b]
    lora_in = pb[:, 3 * d_b:3 * d_b + D_DECAY_LORA + D_AAA_LORA]
    gate_in = pb[:, 3 * d_b + D_DECAY_LORA + D_AAA_LORA:]
    w = -_softplus(-(w0 + _bdot(jnp.tanh(lora_in), w2p))) - 0.5
    decay = jnp.exp(-jnp.exp(w))
    a = jax.nn.sigmoid(a0 + _bdot(lora_in, a2p))
    g = _bdot(jax.nn.sigmoid(gate_in), g2)
    kk = k * kkw
    kk = kk / jnp.maximum(jnp.sqrt(_segsum(kk * kk, ones_blk, 3)), 1e-12)
    k2 = k * (1.0 + (a - 1.0) * kaw)
    bonus = _segsum(r * k2 * rkw, ones_blk, 3) * v
    return r, k2, v, decay, kk, kk * a, g, bonus


def _memkv_kernel(mem_ref, g_ref, wk_ref, wv_ref, mk_ref, mv_ref):
    mn = _rms(mem_ref[...], g_ref[...]).astype(BF16)
    mk_ref[...] = jnp.dot(mn, wk_ref[...], preferred_element_type=F32)
    mv_ref[...] = jnp.dot(mn, wv_ref[...], preferred_element_type=F32)


def _memory_kv(mem2d, g, wk, wv):
    m, d = mem2d.shape
    tm = min(512, m)
    return pl.pallas_call(
        _memkv_kernel,
        grid=(m // tm,),
        in_specs=[pl.BlockSpec((tm, d), lambda i: (i, 0)), _full((1, d)), _full((d, d)), _full((d, d))],
        out_specs=[pl.BlockSpec((tm, d), lambda i: (i, 0))] * 2,
        out_shape=[jax.ShapeDtypeStruct((m, d), F32)] * 2,
        compiler_params=_cparams(("arbitrary",)),
        name="memory_kv",
    )(mem2d, g, wk, wv)


def _inproj_prompt_kernel(x_ref, shift0_ref, g_ref, win_ref, mu_ref, lng_ref, lnb_ref, ws_ref, bsb_ref,
                          w0_ref, w2p_ref, a0_ref, a2p_ref, g2_ref, kkw_ref, kaw_ref, rkw_ref, ones_ref,
                          outa_ref, cv_ref, shift_ref,
                          r_ref, k_ref, v_ref, d_ref, kk_ref, b_ref, gate_ref, bonus_ref,
                          carry_ref, *, tl):
    l = pl.program_id(1)
    last = pl.num_programs(1) - 1
    d_a = N_HEADS_A * HEAD_A

    @pl.when(l == 0)
    def _():
        carry_ref[...] = shift0_ref[0]

    h = _rms(x_ref[0], g_ref[...])
    proj = _bdot(h, win_ref[...])

    z = _gelu(proj[:, :2 * d_a])
    u = z[:, :d_a]
    vn = _headnorm_a(z[:, d_a:], lng_ref[...], lnb_ref[...])
    row = lax.broadcasted_iota(jnp.int32, (CHUNK, CHUNK), 0)
    col = lax.broadcasted_iota(jnp.int32, (CHUNK, CHUNK), 1)
    tril = row >= col
    for h_i in range(N_HEADS_A):
        w_h = jnp.where(tril, ws_ref[h_i], 0.0).astype(BF16)
        lanes = slice(h_i * HEAD_A, (h_i + 1) * HEAD_A)
        for c in range(tl // CHUNK):
            rows = slice(c * CHUNK, (c + 1) * CHUNK)
            mixed = jnp.dot(w_h, vn[rows, lanes].astype(BF16), preferred_element_type=F32) + bsb_ref[h_i]
            outa_ref[0, rows, lanes] = u[rows, lanes] * mixed

    @pl.when(l == last)
    def _():
        cv_ref[0] = vn[tl - CHUNK:, :]

    p = proj[:, 2 * d_a:]
    rolled = pltpu.roll(p, 1, axis=0)
    first = lax.broadcasted_iota(jnp.int32, p.shape, 0) == 0
    prev = jnp.where(first, jnp.broadcast_to(carry_ref[...], p.shape), rolled)
    carry_ref[...] = p[tl - 1:tl, :]

    @pl.when(l == last)
    def _():
        shift_ref[0] = p[tl - 1:tl, :]

    pb = p + (prev - p) * mu_ref[...]
    outs = _rwkv_prep(pb, ones_ref[...], w0_ref[...], w2p_ref[...], a0_ref[...], a2p_ref[...], g2_ref[...],
                      kkw_ref[...], kaw_ref[...], rkw_ref[...])
    for ref, val in zip((r_ref, k_ref, v_ref, d_ref, kk_ref, b_ref, gate_ref, bonus_ref), outs):
        ref[0] = val


def _inproj_prompt(x, shift0, wts, tl):
    bsz, seq, d = x.shape
    d_a = N_HEADS_A * HEAD_A
    d_b = N_HEADS_B * HEAD_B
    d_in = wts["w_in"].shape[1]
    d_shift = d_in - 2 * d_a
    tok = lambda w: pl.BlockSpec((1, tl, w), lambda b, l: (b, l, 0))
    per_b = lambda r, w: pl.BlockSpec((1, r, w), lambda b, l: (b, 0, 0))
    in_specs = [
        tok(d), per_b(1, d_shift), _full((1, d)), _full((d, d_in)), _full((1, d_shift)),
        _full((1, d_a)), _full((1, d_a)), _full((N_HEADS_A, CHUNK, CHUNK)), _full((N_HEADS_A, CHUNK, HEAD_A)),
        _full((1, d_b)), _full((D_DECAY_LORA + D_AAA_LORA, d_b)), _full((1, d_b)),
        _full((D_DECAY_LORA + D_AAA_LORA, d_b)), _full((D_GATE_LORA, d_b)),
        _full((1, d_b)), _full((1, d_b)), _full((1, d_b)), _full((SEG_LANES, SEG_LANES)),
    ]
    out_specs = [tok(d_a), per_b(CHUNK, d_a), per_b(1, d_shift)] + [tok(d_b)] * 8
    out_shape = ([jax.ShapeDtypeStruct((bsz, seq, d_a), F32),
                  jax.ShapeDtypeStruct((bsz, CHUNK, d_a), F32),
                  jax.ShapeDtypeStruct((bsz, 1, d_shift), F32)]
                 + [jax.ShapeDtypeStruct((bsz, seq, d_b), F32)] * 8)
    return pl.pallas_call(
        functools.partial(_inproj_prompt_kernel, tl=tl),
        grid=(bsz, seq // tl),
        in_specs=in_specs, out_specs=out_specs, out_shape=out_shape,
        scratch_shapes=[pltpu.VMEM((1, d_shift), F32)],
        compiler_params=_cparams(("arbitrary", "arbitrary")),
        name="inproj_prompt",
    )(x, shift0, wts["norm_mix_g"], wts["w_in"], wts["mu"], wts["gm_ln_g"], wts["gm_ln_b"], wts["gm_ws"],
      wts["bs_b"], wts["w0"], wts["w2p"], wts["a0"], wts["a2p"], wts["g2"], wts["kkw"], wts["kaw"], wts["rkw"],
      wts["ones"])


def _inproj_sample_kernel(x_ref, shift0_ref, g_ref, win_ref, mu_ref, lng_ref, lnb_ref, ws0_ref, bs0_ref,
                          w0_ref, w2p_ref, a0_ref, a2p_ref, g2_ref, kkw_ref, kaw_ref, rkw_ref, ones_ref,
                          outa_ref, cv_ref, shift_ref,
                          r_ref, k_ref, v_ref, d_ref, kk_ref, b_ref, gate_ref, bonus_ref):
    d_a = N_HEADS_A * HEAD_A
    h = _rms(x_ref[...], g_ref[...])
    proj = _bdot(h, win_ref[...])
    z = _gelu(proj[:, :2 * d_a])
    vn = _headnorm_a(z[:, d_a:], lng_ref[...], lnb_ref[...])
    outa_ref[...] = z[:, :d_a] * (ws0_ref[...] * vn + bs0_ref[...])
    cv_ref[...] = vn
    p = proj[:, 2 * d_a:]
    shift_ref[...] = p
    pb = p + (shift0_ref[...] - p) * mu_ref[...]
    outs = _rwkv_prep(pb, ones_ref[...], w0_ref[...], w2p_ref[...], a0_ref[...], a2p_ref[...], g2_ref[...],
                      kkw_ref[...], kaw_ref[...], rkw_ref[...])
    for ref, val in zip((r_ref, k_ref, v_ref, d_ref, kk_ref, b_ref, gate_ref, bonus_ref), outs):
        ref[...] = val


def _inproj_sample(x, shift0, wts):
    n, d = x.shape
    d_a = N_HEADS_A * HEAD_A
    d_b = N_HEADS_B * HEAD_B
    d_in = wts["w_in"].shape[1]
    d_shift = d_in - 2 * d_a
    args = (x, shift0, wts["norm_mix_g"], wts["w_in"], wts["mu"], wts["gm_ln_g"], wts["gm_ln_b"], wts["ws0"],
            wts["bs0"], wts["w0"], wts["w2p"], wts["a0"], wts["a2p"], wts["g2"], wts["kkw"], wts["kaw"],
            wts["rkw"], wts["ones"])
    out_shape = ([jax.ShapeDtypeStruct((n, d_a), F32), jax.ShapeDtypeStruct((n, d_a), F32),
                  jax.ShapeDtypeStruct((n, d_shift), F32)] + [jax.ShapeDtypeStruct((n, d_b), F32)] * 8)
    return pl.pallas_call(
        _inproj_sample_kernel,
        grid=(1,),
        in_specs=[_full(a.shape) for a in args],
        out_specs=[_full(s.shape) for s in out_shape],
        out_shape=out_shape,
        compiler_params=_cparams(("arbitrary",)),
        name="inproj_sample",
    )(*args)


def _wkv_kernel(r_ref, k_ref, v_ref, d_ref, kk_ref, b_ref, s0_ref, ones_ref, y_ref, sout_ref, s_scr, *, bb, tb):
    l = pl.program_id(1)
    d_b = N_HEADS_B * HEAD_B

    nt = HEAD_B // SUBLANES
    tiled = (nt, SUBLANES, d_b)

    @pl.when(l == 0)
    def _():
        s_scr[...] = s0_ref[...].reshape((bb,) + tiled)

    ones_blk = ones_ref[...]
    v_row = (lax.broadcasted_iota(jnp.int32, tiled, 0) * SUBLANES + lax.broadcasted_iota(jnp.int32, tiled, 1))
    diag = lax.broadcasted_iota(jnp.int32, tiled, 2) % HEAD_B == v_row
    flat = lambda xs: jnp.concatenate([x.reshape(HEAD_B, d_b) for x in xs], axis=0)

    def step(t, carry):
        row = lambda ref, bi: jnp.broadcast_to(ref[bi, pl.ds(t, 1), :], (SUBLANES, d_b))[None]
        p = flat([s_scr[bi] * row(kk_ref, bi) for bi in range(bb)])
        e = flat([jnp.where(diag, row(v_ref, bi), 0.0) for bi in range(bb)])
        sa = _segsum(p, ones_blk, 1)
        vcol = _segsum(e, ones_blk, 1)
        q = []
        for bi in range(bb):
            rows = slice(bi * HEAD_B, (bi + 1) * HEAD_B)
            s_new = (s_scr[bi] * row(d_ref, bi) - sa[rows].reshape(tiled) * row(b_ref, bi)
                     + vcol[rows].reshape(tiled) * row(k_ref, bi))
            s_scr[bi] = s_new
            q.append(s_new * row(r_ref, bi))
        yb = _segsum(flat(q), ones_blk, 1)
        for bi in range(bb):
            y_t = yb[bi * HEAD_B:(bi + 1) * HEAD_B].reshape(tiled)
            picked = jnp.zeros((SUBLANES, d_b), F32)
            for j in range(nt):
                picked = jnp.where(diag[j], y_t[j], picked)
            y_ref[bi, pl.ds(t, 1), :] = jnp.sum(picked, axis=0, keepdims=True)
        return carry

    lax.fori_loop(0, tb, step, 0, unroll=2 if tb % 2 == 0 else 1)

    @pl.when(l == pl.num_programs(1) - 1)
    def _():
        sout_ref[...] = s_scr[...].reshape(bb, HEAD_B, d_b)


def _wkv(r, k, v, d, kk, b, s0, ones_blk, bb, tb):
    bsz, seq, d_b = r.shape
    tok = pl.BlockSpec((bb, tb, d_b), lambda i, l: (i, l, 0))
    st = pl.BlockSpec((bb, HEAD_B, d_b), lambda i, l: (i, 0, 0))
    return pl.pallas_call(
        functools.partial(_wkv_kernel, bb=bb, tb=tb),
        grid=(bsz // bb, seq // tb),
        in_specs=[tok] * 6 + [st, _full((SEG_LANES, SEG_LANES))],
        out_specs=[tok, st],
        out_shape=[jax.ShapeDtypeStruct((bsz, seq, d_b), F32), jax.ShapeDtypeStruct((bsz, HEAD_B, d_b), F32)],
        scratch_shapes=[pltpu.VMEM((bb, HEAD_B // SUBLANES, SUBLANES, d_b), F32)],
        compiler_params=_cparams(("arbitrary", "arbitrary")),
        name="wkv",
    )(r, k, v, d, kk, b, s0, ones_blk)


def _mix_out(x, outa, y, gate, bonus, lng, lnb, wout, ones_blk):
    d_a = outa.shape[-1]
    mu = _segsum(y, ones_blk, 3) * (1.0 / HEAD_B)
    yc = y - mu
    var = _segsum(yc * yc, ones_blk, 3) * (1.0 / HEAD_B)
    yn = yc * lax.rsqrt(var + GN_EPS) * lng + lnb
    outb = (yn + bonus) * gate
    return x + _bdot(outa, wout[:d_a]) + _bdot(outb, wout[d_a:])


def _softmax(s):
    m = jnp.max(s, axis=-1, keepdims=True)
    e = jnp.exp(s - m)
    return e / jnp.sum(e, axis=-1, keepdims=True)


def _mixattn_prompt_kernel(x_ref, outa_ref, y_ref, gate_ref, bonus_ref, mk_ref, mv_ref,
                           lng_ref, lnb_ref, wout_ref, gx_ref, wq_ref, ones_ref, x1_ref, o_ref):
    x1 = _mix_out(x_ref[0], outa_ref[0], y_ref[0], gate_ref[0], bonus_ref[0], lng_ref[...], lnb_ref[...],
                  wout_ref[...], ones_ref[...])
    x1_ref[0] = x1
    q = _bdot(_rms(x1, gx_ref[...]), wq_ref[...])
    hx = q.shape[1] // N_HEADS_X
    for h_i in range(N_HEADS_X):
        lanes = slice(h_i * hx, (h_i + 1) * hx)
        s = lax.dot_general(q[:, lanes].astype(BF16), mk_ref[0, :, lanes], (((1,), (1,)), ((), ())),
                            preferred_element_type=F32) * (hx ** -0.5)
        o_ref[0, :, lanes] = _bdot(_softmax(s), mv_ref[0, :, lanes]).astype(BF16)


def _mixattn_prompt(x, outa, y, gate, bonus, mk, mv, wts, tl):
    bsz, seq, d = x.shape
    d_a = outa.shape[-1]
    d_b = y.shape[-1]
    n_mem = mk.shape[1]
    tok = lambda w: pl.BlockSpec((1, tl, w), lambda b, l: (b, l, 0))
    mem = pl.BlockSpec((1, n_mem, d), lambda b, l: (b, 0, 0))
    return pl.pallas_call(
        _mixattn_prompt_kernel,
        grid=(bsz, seq // tl),
        in_specs=[tok(d), tok(d_a), tok(d_b), tok(d_b), tok(d_b), mem, mem,
                  _full((1, d_b)), _full((1, d_b)), _full((d, d)), _full((1, d)), _full((d, d)),
                  _full((SEG_LANES, SEG_LANES))],
        out_specs=[tok(d), tok(d)],
        out_shape=[jax.ShapeDtypeStruct((bsz, seq, d), F32), jax.ShapeDtypeStruct((bsz, seq, d), BF16)],
        compiler_params=_cparams(("arbitrary", "arbitrary")),
        name="mixattn_prompt",
    )(x, outa, y, gate, bonus, mk, mv, wts["rw_ln_g"], wts["rw_ln_b"], wts["w_out"], wts["norm_x_g"],
      wts["w_xq"], wts["ones"])


def _mixq_sample_kernel(x_ref, outa_ref, y_ref, gate_ref, bonus_ref, lng_ref, lnb_ref, wout_ref, gx_ref, wq_ref,
                        ones_ref, x1_ref, q_ref):
    x1 = _mix_out(x_ref[...], outa_ref[...], y_ref[...], gate_ref[...], bonus_ref[...], lng_ref[...],
                  lnb_ref[...], wout_ref[...], ones_ref[...])
    x1_ref[...] = x1
    q_ref[...] = _bdot(_rms(x1, gx_ref[...]), wq_ref[...])


def _mixq_sample(x, outa, y, gate, bonus, wts):
    n, d = x.shape
    args = (x, outa, y, gate, bonus, wts["rw_ln_g"], wts["rw_ln_b"], wts["w_out"], wts["norm_x_g"], wts["w_xq"],
            wts["ones"])
    return pl.pallas_call(
        _mixq_sample_kernel,
        grid=(1,),
        in_specs=[_full(a.shape) for a in args],
        out_specs=[_full((n, d)), _full((n, d))],
        out_shape=[jax.ShapeDtypeStruct((n, d), F32)] * 2,
        compiler_params=_cparams(("arbitrary",)),
        name="mixq_sample",
    )(*args)


def _decode_attn_kernel(q_ref, k_ref, v_ref, o_ref, *, bb):
    d = q_ref.shape[1]
    hx = d // N_HEADS_X
    head_of_row = lax.broadcasted_iota(jnp.int32, (SUBLANES, d), 0)
    head_of_lane = lax.broadcasted_iota(jnp.int32, (SUBLANES, d), 1) // hx
    head_mask = jnp.where(head_of_row == head_of_lane, 1.0, 0.0).astype(F32)
    for bi in range(bb):
        q_rows = jnp.broadcast_to(q_ref[bi:bi + 1, :], (SUBLANES, d)) * head_mask
        s = lax.dot_general(q_rows.astype(BF16), k_ref[bi].astype(BF16), (((1,), (1,)), ((), ())),
                            preferred_element_type=F32) * (hx ** -0.5)
        o_all = _bdot(_softmax(s), v_ref[bi].astype(BF16))
        o_ref[bi:bi + 1, :] = jnp.sum(o_all * head_mask, axis=0, keepdims=True).astype(BF16)


def _decode_attn(q, mem_k, mem_v, bb):
    n, d = q.shape
    n_mem = mem_k.shape[1]
    kv = pl.BlockSpec((bb, n_mem, d), lambda i: (i, 0, 0))
    return pl.pallas_call(
        functools.partial(_decode_attn_kernel, bb=bb),
        grid=(n // bb,),
        in_specs=[pl.BlockSpec((bb, d), lambda i: (i, 0)), kv, kv],
        out_specs=pl.BlockSpec((bb, d), lambda i: (i, 0)),
        out_shape=jax.ShapeDtypeStruct((n, d), BF16),
        compiler_params=_cparams(("arbitrary",)),
        name="decode_attn",
    )(q, mem_k, mem_v)


def _ffn_kernel(x1_ref, o_ref, wo_ref, gf_ref, wup_ref, wdn_ref, gfin_ref, y_ref, *, ff_chunk):
    x2 = x1_ref[...] + jnp.dot(o_ref[...], wo_ref[...], preferred_element_type=F32)
    hn = _rms(x2, gf_ref[...]).astype(BF16)
    acc = x2
    for c in range(wup_ref.shape[1] // ff_chunk):
        cols = slice(c * ff_chunk, (c + 1) * ff_chunk)
        up = jnp.maximum(jnp.dot(hn, wup_ref[:, cols], preferred_element_type=F32), 0.0)
        acc = acc + _bdot(up * up, wdn_ref[cols, :])
    y_ref[...] = _rms(acc, gfin_ref[...])


def _ffn(x1, o, wts, tm):
    n, d = x1.shape
    d_ff = wts["w_up"].shape[1]
    row = pl.BlockSpec((tm, d), lambda i: (i, 0))
    return pl.pallas_call(
        functools.partial(_ffn_kernel, ff_chunk=min(1024, d_ff)),
        grid=(n // tm,),
        in_specs=[row, row, _full((d, d)), _full((1, d)), _full((d, d_ff)), _full((d_ff, d)), _full((1, d))],
        out_specs=row,
        out_shape=jax.ShapeDtypeStruct((n, d), F32),
        compiler_params=_cparams(("arbitrary",)),
        name="ffn",
    )(x1, o, wts["w_xo"], wts["norm_ffn_g"], wts["w_up"], wts["w_down"], wts["final_g"])


def _state_to_rows(s):
    bsz, nh, nv, nk = s.shape
    return jnp.transpose(s, (0, 2, 1, 3)).reshape(bsz, nv, nh * nk)


def _rows_to_state(s):
    bsz, nv, hk = s.shape
    return jnp.transpose(s.reshape(bsz, nv, N_HEADS_B, hk // N_HEADS_B), (0, 2, 1, 3))


def _layer_weights(l, norm_mix_g, w_in, tshift_mu, gm_ln_g, gm_ln_b, gm_ws, gm_bs, rw_w0, rw_w2, rw_a0, rw_a2,
                   rw_g2, rw_kk, rw_ka, rw_rk, rw_ln_g, rw_ln_b, w_out, norm_x_g, norm_mem_g, w_xq, w_xk, w_xv,
                   w_xo, norm_ffn_g, w_up, w_down, final_g):
    row = lambda a: a.reshape(1, -1).astype(F32)
    d_b = N_HEADS_B * HEAD_B
    zeros_w = jnp.zeros((D_AAA_LORA, d_b), F32)
    zeros_a = jnp.zeros((D_DECAY_LORA, d_b), F32)
    seg = np.arange(SEG_LANES) // HEAD_B
    return {
        "norm_mix_g": row(norm_mix_g[l]), "w_in": w_in[l].astype(BF16), "mu": row(tshift_mu[l]),
        "gm_ln_g": row(gm_ln_g[l]), "gm_ln_b": row(gm_ln_b[l]), "gm_ws": gm_ws[l],
        "bs_b": jnp.broadcast_to(gm_bs[l][:, :, None], (N_HEADS_A, CHUNK, HEAD_A)),
        "ws0": jnp.repeat(gm_ws[l][:, 0, 0], HEAD_A).reshape(1, -1),
        "bs0": jnp.repeat(gm_bs[l][:, 0], HEAD_A).reshape(1, -1),
        "w0": row(rw_w0[l]),
        "w2p": jnp.concatenate([rw_w2[l], zeros_w], axis=0).astype(BF16),
        "a0": row(rw_a0[l]),
        "a2p": jnp.concatenate([zeros_a, rw_a2[l]], axis=0).astype(BF16),
        "g2": rw_g2[l].astype(BF16), "kkw": row(rw_kk[l]), "kaw": row(rw_ka[l]), "rkw": row(rw_rk[l]),
        "rw_ln_g": row(rw_ln_g[l]), "rw_ln_b": row(rw_ln_b[l]), "w_out": w_out[l].astype(BF16),
        "norm_x_g": row(norm_x_g[l]), "norm_mem_g": row(norm_mem_g[l]), "w_xq": w_xq[l].astype(BF16),
        "w_xk": w_xk[l].astype(BF16), "w_xv": w_xv[l].astype(BF16), "w_xo": w_xo[l].astype(BF16),
        "norm_ffn_g": row(norm_ffn_g[l]), "w_up": w_up[l].astype(BF16), "w_down": w_down[l].astype(BF16),
        "final_g": row(final_g),
        "ones": jnp.asarray(seg[:, None] == seg[None, :], F32),
    }


def kernel(x_prompt, x_sample, mem_prompt, cache_mem_k, cache_mem_v, state_shift, state_wkv, norm_mix_g, w_in, tshift_mu, gm_ln_g, gm_ln_b, gm_ws, gm_bs, rw_w0, rw_w2, rw_a0, rw_a2, rw_g2, rw_kk, rw_ka, rw_rk, rw_ln_g, rw_ln_b, w_out, norm_x_g, norm_mem_g, w_xq, w_xk, w_xv, w_xo, norm_ffn_g, w_up, w_down, final_g):
    depth = w_in.shape[0]
    assert depth == 1, "single-layer step"
    bsz, seq, d = x_prompt.shape
    nsmp, dec_seq, _ = x_sample.shape
    assert dec_seq == 1 and seq % CHUNK == 0
    n_mem = mem_prompt.shape[1]
    d_b = N_HEADS_B * HEAD_B
    d_shift = state_shift.shape[-1]
    wts = _layer_weights(0, norm_mix_g, w_in, tshift_mu, gm_ln_g, gm_ln_b, gm_ws, gm_bs, rw_w0, rw_w2, rw_a0,
                         rw_a2, rw_g2, rw_kk, rw_ka, rw_rk, rw_ln_g, rw_ln_b, w_out, norm_x_g, norm_mem_g,
                         w_xq, w_xk, w_xv, w_xo, norm_ffn_g, w_up, w_down, final_g)

    tl = min(256, seq)
    mk, mv = _memory_kv(mem_prompt.reshape(bsz * n_mem, d), wts["norm_mem_g"], wts["w_xk"], wts["w_xv"])
    (outa, cv_p, sh_p, r, k, v, dcy, kk, b, gate, bonus) = _inproj_prompt(
        x_prompt, jnp.zeros((bsz, 1, d_shift), F32), wts, tl)
    wkv_bb = min(8, bsz)
    y, s_p = _wkv(r, k, v, dcy, kk, b, jnp.zeros((bsz, HEAD_B, d_b), F32), wts["ones"], wkv_bb, min(64, seq))
    x1, o = _mixattn_prompt(x_prompt, outa, y, gate, bonus, mk.reshape(bsz, n_mem, d).astype(BF16),
                            mv.reshape(bsz, n_mem, d).astype(BF16), wts, tl)
    y_prompt = _ffn(x1.reshape(bsz * seq, d), o.reshape(bsz * seq, d), wts, min(512, bsz * seq))

    (outa_s, cv_s, sh_s, r, k, v, dcy, kk, b, gate_s, bonus_s) = _inproj_sample(
        x_sample.reshape(nsmp, d), state_shift.reshape(nsmp, d_shift), wts)
    tok = lambda a: a.reshape(nsmp, 1, d_b)
    s0_s = _state_to_rows(state_wkv.reshape((nsmp,) + state_wkv.shape[2:]))
    y_s, s_s = _wkv(tok(r), tok(k), tok(v), tok(dcy), tok(kk), tok(b), s0_s, wts["ones"], min(8, nsmp), 1)
    x1_s, q_s = _mixq_sample(x_sample.reshape(nsmp, d), outa_s, y_s.reshape(nsmp, d_b), gate_s, bonus_s, wts)
    o_s = _decode_attn(q_s, cache_mem_k.reshape(nsmp, n_mem, d), cache_mem_v.reshape(nsmp, n_mem, d),
                       min(8, nsmp))
    y_sample = _ffn(x1_s, o_s, wts, min(512, nsmp))

    mem_shape = (1, bsz, n_mem, N_HEADS_X, d // N_HEADS_X)
    return (y_prompt.reshape(bsz, seq, d), y_sample.reshape(nsmp, 1, d),
            mk.reshape(mem_shape), mv.reshape(mem_shape),
            sh_p.reshape(1, bsz, d_shift), _rows_to_state(s_p)[None],
            cv_p.reshape(1, bsz, CHUNK, N_HEADS_A, HEAD_A),
            sh_s.reshape(1, nsmp, d_shift), _rows_to_state(s_s)[None],
            cv_s.reshape(1, nsmp, 1, N_HEADS_A, HEAD_A))
```

```python
import functools

import numpy as np
import jax
import jax.numpy as jnp
from jax import lax
from jax.experimental import pallas as pl
from jax.experimental.pallas import tpu as pltpu

F32 = jnp.float32
BF16 = jnp.bfloat16

CHUNK = 128
N_HEADS_A = 4
HEAD_A = 128
HEAD_B = 64
N_HEADS_B = 8
D_DECAY_LORA = 64
D_AAA_LORA = 64
D_GATE_LORA = 128
N_HEADS_X = 4
RMS_EPS = 1e-6
LN_EPS = 1e-5
GN_EPS = 64e-5

VMEM_LIMIT_BYTES = 56 * 1024 * 1024
SEG_LANES = 256
SUBLANES = 8
LANES = 128
WKV_GROUP = 8


def _cparams(sem):
    return pltpu.CompilerParams(dimension_semantics=sem, vmem_limit_bytes=VMEM_LIMIT_BYTES)


def _full(shape):
    nd = len(shape)
    return pl.BlockSpec(shape, lambda *_: (0,) * nd)


def _rms(x, g):
    return x * lax.rsqrt(jnp.mean(x * x, axis=-1, keepdims=True) + RMS_EPS) * g


def _bdot(a, w):
    return jnp.dot(a.astype(BF16), w, preferred_element_type=F32)


def _segsum(x, ones_blk, passes):
    outs = []
    for j in range(x.shape[1] // SEG_LANES):
        rem = x[:, j * SEG_LANES:(j + 1) * SEG_LANES]
        acc = None
        for p in range(passes):
            piece = rem if p + 1 == passes else rem.astype(BF16).astype(F32)
            t = jnp.dot(piece, ones_blk, preferred_element_type=F32)
            acc = t if acc is None else acc + t
            if p + 1 < passes:
                rem = rem - piece
        outs.append(acc)
    return jnp.concatenate(outs, axis=1)


def _gelu(x):
    return 0.5 * x * (1.0 + lax.erf(x * np.float32(np.sqrt(0.5))))


def _softplus(x):
    return jnp.maximum(x, 0.0) + jnp.log1p(jnp.exp(-jnp.abs(x)))


def _headnorm_a(z, g, b):
    outs = []
    for h in range(N_HEADS_A):
        zh = z[:, h * HEAD_A:(h + 1) * HEAD_A]
        mu = jnp.mean(zh, axis=-1, keepdims=True)
        zc = zh - mu
        var = jnp.mean(zc * zc, axis=-1, keepdims=True)
        outs.append(zc * lax.rsqrt(var + LN_EPS))
    return jnp.concatenate(outs, axis=1) * g + b


def _rwkv_prep(pb, ones_blk, w0, w2p, a0, a2p, g2, kkw, kaw, rkw):
    d_b = N_HEADS_B * HEAD_B
    r = pb[:, 0:d_b]
    k = pb[:, d_b:2 * d_b]
    v = pb[:, 2 * d_b:3 * d_b]
    lora_in = pb[:, 3 * d_b:3 * d_b + D_DECAY_LORA + D_AAA_LORA]
    gate_in = pb[:, 3 * d_b + D_DECAY_LORA + D_AAA_LORA:]
    w = -_softplus(-(w0 + _bdot(jnp.tanh(lora_in), w2p))) - 0.5
    decay = jnp.exp(-jnp.exp(w))
    a = jax.nn.sigmoid(a0 + _bdot(lora_in, a2p))
    g = _bdot(jax.nn.sigmoid(gate_in), g2)
    kk = k * kkw
    kk = kk / jnp.maximum(jnp.sqrt(_segsum(kk * kk, ones_blk, 3)), 1e-12)
    k2 = k * (1.0 + (a - 1.0) * kaw)
    bonus = _segsum(r * k2 * rkw, ones_blk, 3) * v
    return r, k2, v, decay, kk, kk * a, g, bonus


def _memkv_kernel(mem_ref, g_ref, wk_ref, wv_ref, mk_ref, mv_ref, mkb_ref, mvb_ref):
    mn = _rms(mem_ref[...], g_ref[...]).astype(BF16)
    for w_ref, out_ref, outb_ref in ((wk_ref, mk_ref, mkb_ref), (wv_ref, mv_ref, mvb_ref)):
        res = jnp.dot(mn, w_ref[...], preferred_element_type=F32)
        out_ref[...] = res.reshape(out_ref.shape)
        outb_ref[...] = res.astype(BF16)


def _memory_kv(mem2d, g, wk, wv):
    m, d = mem2d.shape
    tm = min(512, m)
    hx = d // N_HEADS_X
    return pl.pallas_call(
        _memkv_kernel,
        grid=(m // tm,),
        in_specs=[pl.BlockSpec((tm, d), lambda i: (i, 0)), _full((1, d)), _full((d, d)), _full((d, d))],
        out_specs=[pl.BlockSpec((tm, N_HEADS_X, hx), lambda i: (i, 0, 0))] * 2
        + [pl.BlockSpec((tm, d), lambda i: (i, 0))] * 2,
        out_shape=[jax.ShapeDtypeStruct((m, N_HEADS_X, hx), F32)] * 2 + [jax.ShapeDtypeStruct((m, d), BF16)] * 2,
        compiler_params=_cparams(("arbitrary",)),
        name="memory_kv",
    )(mem2d, g, wk, wv)


def _inproj_prompt_kernel(x_ref, shift0_ref, g_ref, win_ref, mu_ref, lng_ref, lnb_ref, ws_ref, bsb_ref,
                          w0_ref, w2p_ref, a0_ref, a2p_ref, g2_ref, kkw_ref, kaw_ref, rkw_ref, ones_ref,
                          outa_ref, cv_ref, shift_ref,
                          r_ref, k_ref, v_ref, d_ref, kk_ref, b_ref, gate_ref, bonus_ref,
                          carry_ref, *, tl):
    l = pl.program_id(1)
    last = pl.num_programs(1) - 1
    d_a = N_HEADS_A * HEAD_A

    @pl.when(l == 0)
    def _():
        carry_ref[...] = shift0_ref[0]

    h = _rms(x_ref[0], g_ref[...])
    proj = _bdot(h, win_ref[...])

    z = _gelu(proj[:, :2 * d_a])
    u = z[:, :d_a]
    vn = _headnorm_a(z[:, d_a:], lng_ref[...], lnb_ref[...])
    row = lax.broadcasted_iota(jnp.int32, (CHUNK, CHUNK), 0)
    col = lax.broadcasted_iota(jnp.int32, (CHUNK, CHUNK), 1)
    tril = row >= col
    for h_i in range(N_HEADS_A):
        w_h = jnp.where(tril, ws_ref[h_i], 0.0).astype(BF16)
        lanes = slice(h_i * HEAD_A, (h_i + 1) * HEAD_A)
        for c in range(tl // CHUNK):
            rows = slice(c * CHUNK, (c + 1) * CHUNK)
            mixed = jnp.dot(w_h, vn[rows, lanes].astype(BF16), preferred_element_type=F32) + bsb_ref[h_i]
            outa_ref[0, rows, lanes] = u[rows, lanes] * mixed

    @pl.when(l == last)
    def _():
        cv_ref[0] = vn[tl - CHUNK:, :].reshape(CHUNK, N_HEADS_A, HEAD_A)

    p = proj[:, 2 * d_a:]
    rolled = pltpu.roll(p, 1, axis=0)
    first = lax.broadcasted_iota(jnp.int32, p.shape, 0) == 0
    prev = jnp.where(first, jnp.broadcast_to(carry_ref[...], p.shape), rolled)
    carry_ref[...] = p[tl - 1:tl, :]

    @pl.when(l == last)
    def _():
        shift_ref[0] = p[tl - 1:tl, :]

    pb = p + (prev - p) * mu_ref[...]
    outs = _rwkv_prep(pb, ones_ref[...], w0_ref[...], w2p_ref[...], a0_ref[...], a2p_ref[...], g2_ref[...],
                      kkw_ref[...], kaw_ref[...], rkw_ref[...])
    for ref, val in zip((r_ref, k_ref, v_ref, d_ref, kk_ref, b_ref, gate_ref, bonus_ref), outs):
        ref[0] = val


def _inproj_prompt(x, shift0, wts, tl):
    bsz, seq, d = x.shape
    d_a = N_HEADS_A * HEAD_A
    d_b = N_HEADS_B * HEAD_B
    d_in = wts["w_in"].shape[1]
    d_shift = d_in - 2 * d_a
    tok = lambda w: pl.BlockSpec((1, tl, w), lambda b, l: (b, l, 0))
    per_b = lambda r, w: pl.BlockSpec((1, r, w), lambda b, l: (b, 0, 0))
    in_specs = [
        tok(d), per_b(1, d_shift), _full((1, d)), _full((d, d_in)), _full((1, d_shift)),
        _full((1, d_a)), _full((1, d_a)), _full((N_HEADS_A, CHUNK, CHUNK)), _full((N_HEADS_A, CHUNK, HEAD_A)),
        _full((1, d_b)), _full((D_DECAY_LORA + D_AAA_LORA, d_b)), _full((1, d_b)),
        _full((D_DECAY_LORA + D_AAA_LORA, d_b)), _full((D_GATE_LORA, d_b)),
        _full((1, d_b)), _full((1, d_b)), _full((1, d_b)), _full((SEG_LANES, SEG_LANES)),
    ]
    chunk_v = pl.BlockSpec((1, CHUNK, N_HEADS_A, HEAD_A), lambda b, l: (b, 0, 0, 0))
    out_specs = [tok(d_a), chunk_v, per_b(1, d_shift)] + [tok(d_b)] * 8
    out_shape = ([jax.ShapeDtypeStruct((bsz, seq, d_a), F32),
                  jax.ShapeDtypeStruct((bsz, CHUNK, N_HEADS_A, HEAD_A), F32),
                  jax.ShapeDtypeStruct((bsz, 1, d_shift), F32)]
                 + [jax.ShapeDtypeStruct((bsz, seq, d_b), F32)] * 8)
    return pl.pallas_call(
        functools.partial(_inproj_prompt_kernel, tl=tl),
        grid=(bsz, seq // tl),
        in_specs=in_specs, out_specs=out_specs, out_shape=out_shape,
        scratch_shapes=[pltpu.VMEM((1, d_shift), F32)],
        compiler_params=_cparams(("arbitrary", "arbitrary")),
        name="inproj_prompt",
    )(x, shift0, wts["norm_mix_g"], wts["w_in"], wts["mu"], wts["gm_ln_g"], wts["gm_ln_b"], wts["gm_ws"],
      wts["bs_b"], wts["w0"], wts["w2p"], wts["a0"], wts["a2p"], wts["g2"], wts["kkw"], wts["kaw"], wts["rkw"],
      wts["ones"])


def _inproj_sample_kernel(x_ref, shift0_ref, g_ref, win_ref, mu_ref, lng_ref, lnb_ref, ws0_ref, bs0_ref,
                          w0_ref, w2p_ref, a0_ref, a2p_ref, g2_ref, kkw_ref, kaw_ref, rkw_ref, ones_ref,
                          outa_ref, cv_ref, shift_ref,
                          r_ref, k_ref, v_ref, d_ref, kk_ref, b_ref, gate_ref, bonus_ref):
    d_a = N_HEADS_A * HEAD_A
    h = _rms(x_ref[...], g_ref[...])
    proj = _bdot(h, win_ref[...])
    z = _gelu(proj[:, :2 * d_a])
    vn = _headnorm_a(z[:, d_a:], lng_ref[...], lnb_ref[...])
    outa_ref[...] = z[:, :d_a] * (ws0_ref[...] * vn + bs0_ref[...])
    cv_ref[...] = vn.reshape(cv_ref.shape)
    p = proj[:, 2 * d_a:]
    shift_ref[...] = p
    pb = p + (shift0_ref[...] - p) * mu_ref[...]
    outs = _rwkv_prep(pb, ones_ref[...], w0_ref[...], w2p_ref[...], a0_ref[...], a2p_ref[...], g2_ref[...],
                      kkw_ref[...], kaw_ref[...], rkw_ref[...])
    for ref, val in zip((r_ref, k_ref, v_ref, d_ref, kk_ref, b_ref, gate_ref, bonus_ref), outs):
        ref[...] = val


def _inproj_sample(x, shift0, wts):
    n, d = x.shape
    d_a = N_HEADS_A * HEAD_A
    d_b = N_HEADS_B * HEAD_B
    d_in = wts["w_in"].shape[1]
    d_shift = d_in - 2 * d_a
    args = (x, shift0, wts["norm_mix_g"], wts["w_in"], wts["mu"], wts["gm_ln_g"], wts["gm_ln_b"], wts["ws0"],
            wts["bs0"], wts["w0"], wts["w2p"], wts["a0"], wts["a2p"], wts["g2"], wts["kkw"], wts["kaw"],
            wts["rkw"], wts["ones"])
    out_shape = ([jax.ShapeDtypeStruct((n, d_a), F32), jax.ShapeDtypeStruct((n, N_HEADS_A, HEAD_A), F32),
                  jax.ShapeDtypeStruct((n, d_shift), F32)] + [jax.ShapeDtypeStruct((n, d_b), F32)] * 8)
    return pl.pallas_call(
        _inproj_sample_kernel,
        grid=(1,),
        in_specs=[_full(a.shape) for a in args],
        out_specs=[_full(s.shape) for s in out_shape],
        out_shape=out_shape,
        compiler_params=_cparams(("arbitrary",)),
        name="inproj_sample",
    )(*args)


def _wkv_kernel(r_ref, k_ref, v_ref, d_ref, kk_ref, b_ref, s0_ref, ones_ref, y_ref, sout_ref, s_scr, y_scr,
                *, bb, tb):
    l = pl.program_id(1)
    d_b = N_HEADS_B * HEAD_B
    assert tb <= HEAD_B

    nt = HEAD_B // SUBLANES
    tiled = (nt, SUBLANES, d_b)

    @pl.when(l == 0)
    def _():
        heads = [s0_ref[:, h] for h in range(N_HEADS_B)]
        s_scr[...] = jnp.concatenate(heads, axis=-1).reshape((bb,) + tiled)

    if tb < HEAD_B:
        y_scr[...] = jnp.zeros(y_scr.shape, F32)

    ones_blk = ones_ref[...]
    lane_in_head = lax.broadcasted_iota(jnp.int32, (SUBLANES, d_b), 1) % HEAD_B
    v_row = (lax.broadcasted_iota(jnp.int32, tiled, 0) * SUBLANES + lax.broadcasted_iota(jnp.int32, tiled, 1))
    diag = lax.broadcasted_iota(jnp.int32, (nt, SUBLANES, LANES), 2) % HEAD_B == v_row[..., :LANES]
    grp = min(bb, WKV_GROUP)
    flat = lambda xs: jnp.concatenate([x.reshape(HEAD_B, d_b) for x in xs], axis=0)

    def step(t, carry):
        row = lambda ref, bi: jnp.broadcast_to(ref[bi, pl.ds(t, 1), :], (SUBLANES, d_b))[None]
        slot = (lane_in_head == t)[None]
        for g0 in range(0, bb, grp):
            group = range(g0, g0 + grp)
            part = lambda x, n: x[n * HEAD_B:(n + 1) * HEAD_B].reshape(tiled)
            lhs = []
            for bi in group:
                v_t = row(v_ref, bi)
                v_diag = jnp.concatenate([jnp.where(diag, v_t[..., c * LANES:(c + 1) * LANES], 0.0)
                                          for c in range(d_b // LANES)], axis=-1)
                lhs += [s_scr[bi] * row(kk_ref, bi), v_diag]
            cols = _segsum(flat(lhs), ones_blk, 1)
            q = []
            for n, bi in enumerate(group):
                s_new = (s_scr[bi] * row(d_ref, bi) - part(cols, 2 * n) * row(b_ref, bi)
                         + part(cols, 2 * n + 1) * row(k_ref, bi))
                s_scr[bi] = s_new
                q.append(s_new * row(r_ref, bi))
            yb = _segsum(flat(q), ones_blk, 1)
            for n, bi in enumerate(group):
                y_scr[bi] = jnp.where(slot, part(yb, n), y_scr[bi])
        return carry

    lax.fori_loop(0, tb, step, 0, unroll=4 if tb % 4 == 0 else 1)

    left_head = lax.broadcasted_iota(jnp.int32, (HEAD_B, 2 * HEAD_B), 1) < HEAD_B
    for bi in range(bb):
        for pr in range(d_b // (2 * HEAD_B)):
            lanes = slice(pr * 2 * HEAD_B, (pr + 1) * 2 * HEAD_B)
            blk = y_scr[bi, :, :, lanes].reshape(HEAD_B, 2 * HEAD_B)
            tr = jnp.concatenate([blk, jnp.zeros_like(blk)], axis=0).T
            rows_t = jnp.where(left_head, tr[:HEAD_B], pltpu.roll(tr[HEAD_B:], HEAD_B, axis=1))
            y_ref[bi, :, lanes] = rows_t[:tb]

    @pl.when(l == pl.num_programs(1) - 1)
    def _():
        s_fin = s_scr[...].reshape(bb, HEAD_B, d_b)
        for h in range(N_HEADS_B):
            sout_ref[:, h] = s_fin[:, :, h * HEAD_B:(h + 1) * HEAD_B]


def _wkv(r, k, v, d, kk, b, s0, ones_blk, bb, tb):
    bsz, seq, d_b = r.shape
    tok = pl.BlockSpec((bb, tb, d_b), lambda i, l: (i, l, 0))
    st = pl.BlockSpec((bb, N_HEADS_B, HEAD_B, HEAD_B), lambda i, l: (i, 0, 0, 0))
    return pl.pallas_call(
        functools.partial(_wkv_kernel, bb=bb, tb=tb),
        grid=(bsz // bb, seq // tb),
        in_specs=[tok] * 6 + [st, _full((SEG_LANES, SEG_LANES))],
        out_specs=[tok, st],
        out_shape=[jax.ShapeDtypeStruct((bsz, seq, d_b), F32), jax.ShapeDtypeStruct(s0.shape, F32)],
        scratch_shapes=[pltpu.VMEM((bb, HEAD_B // SUBLANES, SUBLANES, d_b), F32)] * 2,
        compiler_params=_cparams(("arbitrary", "arbitrary")),
        name="wkv",
    )(r, k, v, d, kk, b, s0, ones_blk)


def _mix_out(x, outa, y, gate, bonus, lng, lnb, wout, ones_blk):
    d_a = outa.shape[-1]
    mu = _segsum(y, ones_blk, 3) * (1.0 / HEAD_B)
    yc = y - mu
    var = _segsum(yc * yc, ones_blk, 3) * (1.0 / HEAD_B)
    yn = yc * lax.rsqrt(var + GN_EPS) * lng + lnb
    outb = (yn + bonus) * gate
    return x + _bdot(outa, wout[:d_a]) + _bdot(outb, wout[d_a:])


def _softmax(s):
    m = jnp.max(s, axis=-1, keepdims=True)
    e = jnp.exp(s - m)
    return e / jnp.sum(e, axis=-1, keepdims=True)


def _mixattn_prompt_kernel(x_ref, outa_ref, y_ref, gate_ref, bonus_ref, mk_ref, mv_ref,
                           lng_ref, lnb_ref, wout_ref, gx_ref, wq_ref, ones_ref, x1_ref, o_ref):
    x1 = _mix_out(x_ref[0], outa_ref[0], y_ref[0], gate_ref[0], bonus_ref[0], lng_ref[...], lnb_ref[...],
                  wout_ref[...], ones_ref[...])
    x1_ref[0] = x1
    q = _bdot(_rms(x1, gx_ref[...]), wq_ref[...])
    hx = q.shape[1] // N_HEADS_X
    for h_i in range(N_HEADS_X):
        lanes = slice(h_i * hx, (h_i + 1) * hx)
        s = lax.dot_general(q[:, lanes].astype(BF16), mk_ref[0, :, lanes], (((1,), (1,)), ((), ())),
                            preferred_element_type=F32) * (hx ** -0.5)
        o_ref[0, :, lanes] = _bdot(_softmax(s), mv_ref[0, :, lanes]).astype(BF16)


def _mixattn_prompt(x, outa, y, gate, bonus, mk, mv, wts, tl):
    bsz, seq, d = x.shape
    d_a = outa.shape[-1]
    d_b = y.shape[-1]
    n_mem = mk.shape[1]
    tok = lambda w: pl.BlockSpec((1, tl, w), lambda b, l: (b, l, 0))
    mem = pl.BlockSpec((1, n_mem, d), lambda b, l: (b, 0, 0))
    return pl.pallas_call(
        _mixattn_prompt_kernel,
        grid=(bsz, seq // tl),
        in_specs=[tok(d), tok(d_a), tok(d_b), tok(d_b), tok(d_b), mem, mem,
                  _full((1, d_b)), _full((1, d_b)), _full((d, d)), _full((1, d)), _full((d, d)),
                  _full((SEG_LANES, SEG_LANES))],
        out_specs=[tok(d), tok(d)],
        out_shape=[jax.ShapeDtypeStruct((bsz, seq, d), F32), jax.ShapeDtypeStruct((bsz, seq, d), BF16)],
        compiler_params=_cparams(("arbitrary", "arbitrary")),
        name="mixattn_prompt",
    )(x, outa, y, gate, bonus, mk, mv, wts["rw_ln_g"], wts["rw_ln_b"], wts["w_out"], wts["norm_x_g"],
      wts["w_xq"], wts["ones"])


def _mixq_sample_kernel(x_ref, outa_ref, y_ref, gate_ref, bonus_ref, lng_ref, lnb_ref, wout_ref, gx_ref, wq_ref,
                        ones_ref, x1_ref, q_ref):
    x1 = _mix_out(x_ref[...], outa_ref[...], y_ref[...], gate_ref[...], bonus_ref[...], lng_ref[...],
                  lnb_ref[...], wout_ref[...], ones_ref[...])
    x1_ref[...] = x1
    q_ref[...] = _bdot(_rms(x1, gx_ref[...]), wq_ref[...])


def _mixq_sample(x, outa, y, gate, bonus, wts):
    n, d = x.shape
    args = (x, outa, y, gate, bonus, wts["rw_ln_g"], wts["rw_ln_b"], wts["w_out"], wts["norm_x_g"], wts["w_xq"],
            wts["ones"])
    return pl.pallas_call(
        _mixq_sample_kernel,
        grid=(1,),
        in_specs=[_full(a.shape) for a in args],
        out_specs=[_full((n, d)), _full((n, d))],
        out_shape=[jax.ShapeDtypeStruct((n, d), F32)] * 2,
        compiler_params=_cparams(("arbitrary",)),
        name="mixq_sample",
    )(*args)


def _decode_attn_kernel(q_ref, k_ref, v_ref, o_ref, *, bb):
    d = q_ref.shape[1]
    hx = d // N_HEADS_X
    n_rows = k_ref.shape[1] * N_HEADS_X
    head_of_q = lax.broadcasted_iota(jnp.int32, (2 * N_HEADS_X, n_rows), 0) % N_HEADS_X
    head_of_kv = lax.broadcasted_iota(jnp.int32, (2 * N_HEADS_X, n_rows), 1) % N_HEADS_X
    own = head_of_q == head_of_kv
    for bi in range(bb):
        q_heads = [q_ref[bi:bi + 1, h * hx:(h + 1) * hx] for h in range(N_HEADS_X)]
        q_rows = jnp.concatenate(q_heads + q_heads, axis=0)
        k_rows = k_ref[bi].reshape(n_rows, hx).astype(BF16)
        s = lax.dot_general(q_rows.astype(BF16), k_rows, (((1,), (1,)), ((), ())),
                            preferred_element_type=F32) * (hx ** -0.5)
        p = _softmax(jnp.where(own, s, -1e30))
        o_heads = _bdot(p, v_ref[bi].reshape(n_rows, hx).astype(BF16))
        for h in range(N_HEADS_X):
            o_ref[bi:bi + 1, h * hx:(h + 1) * hx] = o_heads[h:h + 1].astype(BF16)


def _decode_attn(q, mem_k, mem_v, bb):
    n, d = q.shape
    n_mem, nh, hx = mem_k.shape[1:]
    kv = pl.BlockSpec((bb, n_mem, nh, hx), lambda i: (i, 0, 0, 0))
    return pl.pallas_call(
        functools.partial(_decode_attn_kernel, bb=bb),
        grid=(n // bb,),
        in_specs=[pl.BlockSpec((bb, d), lambda i: (i, 0)), kv, kv],
        out_specs=pl.BlockSpec((bb, d), lambda i: (i, 0)),
        out_shape=jax.ShapeDtypeStruct((n, d), BF16),
        compiler_params=_cparams(("arbitrary",)),
        name="decode_attn",
    )(q, mem_k, mem_v)


def _ffn_kernel(x1_ref, o_ref, wo_ref, gf_ref, wup_ref, wdn_ref, gfin_ref, y_ref, *, ff_chunk):
    x2 = x1_ref[...] + jnp.dot(o_ref[...], wo_ref[...], preferred_element_type=F32)
    hn = _rms(x2, gf_ref[...]).astype(BF16)
    acc = x2
    for c in range(wup_ref.shape[1] // ff_chunk):
        cols = slice(c * ff_chunk, (c + 1) * ff_chunk)
        up = jnp.maximum(jnp.dot(hn, wup_ref[:, cols], preferred_element_type=F32), 0.0)
        acc = acc + _bdot(up * up, wdn_ref[cols, :])
    y_ref[...] = _rms(acc, gfin_ref[...])


def _ffn(x1, o, wts, tm):
    n, d = x1.shape
    d_ff = wts["w_up"].shape[1]
    row = pl.BlockSpec((tm, d), lambda i: (i, 0))
    return pl.pallas_call(
        functools.partial(_ffn_kernel, ff_chunk=min(1024, d_ff)),
        grid=(n // tm,),
        in_specs=[row, row, _full((d, d)), _full((1, d)), _full((d, d_ff)), _full((d_ff, d)), _full((1, d))],
        out_specs=row,
        out_shape=jax.ShapeDtypeStruct((n, d), F32),
        compiler_params=_cparams(("arbitrary",)),
        name="ffn",
    )(x1, o, wts["w_xo"], wts["norm_ffn_g"], wts["w_up"], wts["w_down"], wts["final_g"])


def _layer_weights(l, norm_mix_g, w_in, tshift_mu, gm_ln_g, gm_ln_b, gm_ws, gm_bs, rw_w0, rw_w2, rw_a0, rw_a2,
                   rw_g2, rw_kk, rw_ka, rw_rk, rw_ln_g, rw_ln_b, w_out, norm_x_g, norm_mem_g, w_xq, w_xk, w_xv,
                   w_xo, norm_ffn_g, w_up, w_down, final_g):
    row = lambda a: a.reshape(1, -1).astype(F32)
    d_b = N_HEADS_B * HEAD_B
    zeros_w = jnp.zeros((D_AAA_LORA, d_b), F32)
    zeros_a = jnp.zeros((D_DECAY_LORA, d_b), F32)
    seg = np.arange(SEG_LANES) // HEAD_B
    return {
        "norm_mix_g": row(norm_mix_g[l]), "w_in": w_in[l].astype(BF16), "mu": row(tshift_mu[l]),
        "gm_ln_g": row(gm_ln_g[l]), "gm_ln_b": row(gm_ln_b[l]), "gm_ws": gm_ws[l],
        "bs_b": jnp.broadcast_to(gm_bs[l][:, :, None], (N_HEADS_A, CHUNK, HEAD_A)),
        "ws0": jnp.repeat(gm_ws[l][:, 0, 0], HEAD_A).reshape(1, -1),
        "bs0": jnp.repeat(gm_bs[l][:, 0], HEAD_A).reshape(1, -1),
        "w0": row(rw_w0[l]),
        "w2p": jnp.concatenate([rw_w2[l], zeros_w], axis=0).astype(BF16),
        "a0": row(rw_a0[l]),
        "a2p": jnp.concatenate([zeros_a, rw_a2[l]], axis=0).astype(BF16),
        "g2": rw_g2[l].astype(BF16), "kkw": row(rw_kk[l]), "kaw": row(rw_ka[l]), "rkw": row(rw_rk[l]),
        "rw_ln_g": row(rw_ln_g[l]), "rw_ln_b": row(rw_ln_b[l]), "w_out": w_out[l].astype(BF16),
        "norm_x_g": row(norm_x_g[l]), "norm_mem_g": row(norm_mem_g[l]), "w_xq": w_xq[l].astype(BF16),
        "w_xk": w_xk[l].astype(BF16), "w_xv": w_xv[l].astype(BF16), "w_xo": w_xo[l].astype(BF16),
        "norm_ffn_g": row(norm_ffn_g[l]), "w_up": w_up[l].astype(BF16), "w_down": w_down[l].astype(BF16),
        "final_g": row(final_g),
        "ones": jnp.asarray(seg[:, None] == seg[None, :], F32),
    }


def kernel(x_prompt, x_sample, mem_prompt, cache_mem_k, cache_mem_v, state_shift, state_wkv, norm_mix_g, w_in, tshift_mu, gm_ln_g, gm_ln_b, gm_ws, gm_bs, rw_w0, rw_w2, rw_a0, rw_a2, rw_g2, rw_kk, rw_ka, rw_rk, rw_ln_g, rw_ln_b, w_out, norm_x_g, norm_mem_g, w_xq, w_xk, w_xv, w_xo, norm_ffn_g, w_up, w_down, final_g):
    depth = w_in.shape[0]
    assert depth == 1, "single-layer step"
    bsz, seq, d = x_prompt.shape
    nsmp, dec_seq, _ = x_sample.shape
    assert dec_seq == 1 and seq % CHUNK == 0
    n_mem = mem_prompt.shape[1]
    d_b = N_HEADS_B * HEAD_B
    d_shift = state_shift.shape[-1]
    wts = _layer_weights(0, norm_mix_g, w_in, tshift_mu, gm_ln_g, gm_ln_b, gm_ws, gm_bs, rw_w0, rw_w2, rw_a0,
                         rw_a2, rw_g2, rw_kk, rw_ka, rw_rk, rw_ln_g, rw_ln_b, w_out, norm_x_g, norm_mem_g,
                         w_xq, w_xk, w_xv, w_xo, norm_ffn_g, w_up, w_down, final_g)

    tl = min(256, seq)
    mk, mv, mk_b, mv_b = _memory_kv(mem_prompt.reshape(bsz * n_mem, d), wts["norm_mem_g"], wts["w_xk"],
                                    wts["w_xv"])
    (outa, cv_p, sh_p, r, k, v, dcy, kk, b, gate, bonus) = _inproj_prompt(
        x_prompt, jnp.zeros((bsz, 1, d_shift), F32), wts, tl)
    wkv_bb = min(8, bsz)
    y, s_p = _wkv(r, k, v, dcy, kk, b, jnp.zeros((bsz,) + state_wkv.shape[2:], F32), wts["ones"], wkv_bb,
                  min(64, seq))
    x1, o = _mixattn_prompt(x_prompt, outa, y, gate, bonus, mk_b.reshape(bsz, n_mem, d),
                            mv_b.reshape(bsz, n_mem, d), wts, tl)
    y_prompt = _ffn(x1.reshape(bsz * seq, d), o.reshape(bsz * seq, d), wts, min(512, bsz * seq))

    (outa_s, cv_s, sh_s, r, k, v, dcy, kk, b, gate_s, bonus_s) = _inproj_sample(
        x_sample.reshape(nsmp, d), state_shift.reshape(nsmp, d_shift), wts)
    tok = lambda a: a.reshape(nsmp, 1, d_b)
    s0_s = state_wkv.reshape((nsmp,) + state_wkv.shape[2:])
    y_s, s_s = _wkv(tok(r), tok(k), tok(v), tok(dcy), tok(kk), tok(b), s0_s, wts["ones"], min(8, nsmp), 1)
    x1_s, q_s = _mixq_sample(x_sample.reshape(nsmp, d), outa_s, y_s.reshape(nsmp, d_b), gate_s, bonus_s, wts)
    kv_shape = (nsmp,) + cache_mem_k.shape[2:]
    o_s = _decode_attn(q_s, cache_mem_k.reshape(kv_shape), cache_mem_v.reshape(kv_shape), min(8, nsmp))
    y_sample = _ffn(x1_s, o_s, wts, min(512, nsmp))

    mem_shape = (1, bsz, n_mem, N_HEADS_X, d // N_HEADS_X)
    return (y_prompt.reshape(bsz, seq, d), y_sample.reshape(nsmp, 1, d),
            mk.reshape(mem_shape), mv.reshape(mem_shape),
            sh_p.reshape(1, bsz, d_shift), s_p[None],
            cv_p.reshape(1, bsz, CHUNK, N_HEADS_A, HEAD_A),
            sh_s.reshape(1, nsmp, d_shift), s_s[None],
            cv_s.reshape(1, nsmp, 1, N_HEADS_A, HEAD_A))
```
